```python
import math
import jax
import jax.numpy as jnp
from jax import lax
import numpy as np

D_MODEL = 2048
BATCH = 8
SEQ = 2048
DEPTH = 2

D_FF = ((8 * D_MODEL // 3 + 255) // 256) * 256
N_MOD = 9
POOL_WIDTH = D_MODEL // 2
POOL_WINDOWS = (2, 4, 8, 16)
POOL_GROUPS = len(POOL_WINDOWS)
POOL_GROUP_WIDTH = POOL_WIDTH // POOL_GROUPS
FOX_HEAD_DIM = 128
FOX_WIDTH = D_MODEL // 2
FOX_HEADS = FOX_WIDTH // FOX_HEAD_DIM
Q_BLOCK = 128
AB_IN = POOL_WIDTH + 3 * FOX_WIDTH + FOX_HEADS
AB_OUT = POOL_WIDTH + FOX_WIDTH
GDN_HEAD_DIM = 128
GDN_QK_HEADS = D_MODEL // GDN_HEAD_DIM
GDN_V_HEADS = 2 * GDN_QK_HEADS
GDN_QK_DIM = GDN_QK_HEADS * GDN_HEAD_DIM
GDN_V_DIM = GDN_V_HEADS * GDN_HEAD_DIM
GDN_CONV_CH = 2 * GDN_QK_DIM + GDN_V_DIM
GDN_IN = GDN_CONV_CH + GDN_V_DIM + 2 * GDN_V_HEADS
CONV_K = 4
CHUNK = 64
N_EVEN = (DEPTH + 1) // 2
N_ODD = DEPTH // 2
EPS = 1e-6

kernel_name = "hybrid_pool_fox_gdn_macaron_adaln"


def rms_norm(x, w):
    xf = x.astype(jnp.float32)
    y = xf * lax.rsqrt(jnp.mean(xf * xf, axis=-1, keepdims=True) + EPS)
    return (y * w.astype(jnp.float32)).astype(x.dtype)


def l2norm(x):
    return x * lax.rsqrt(jnp.sum(x * x, axis=-1, keepdims=True) + EPS)


def modulate(h, shift, scale):
    return h * (1.0 + scale[:, None, :]) + shift[:, None, :]


def swiglu(h, w_gu, w_down):
    gate, up = jnp.split(h @ w_gu, 2, axis=-1)
    return (jax.nn.silu(gate) * up) @ w_down


def multiscale_pool(u, pool_w, pool_scale):
    B, S, _ = u.shape
    ug = u.reshape(B, S, POOL_GROUPS, POOL_GROUP_WIDTH)
    cs = jnp.cumsum(ug.astype(jnp.float32), axis=1)
    pos = jnp.arange(S)
    outs = []
    for g, win in enumerate(POOL_WINDOWS):
        csg = cs[:, :, g]
        lag = jnp.pad(csg, ((0, 0), (win, 0), (0, 0)))[:, :S]
        count = jnp.minimum(pos + 1, win).astype(jnp.float32)
        mean = (csg - lag) / count[None, :, None]
        outs.append(mean.astype(u.dtype) - ug[:, :, g])
    d = jnp.stack(outs, axis=2)
    y = jnp.einsum('bsgc,gcd->bsgd', d, pool_w)
    return y.reshape(B, S, POOL_WIDTH) * pool_scale


def forgetting_attention(q, k, v, log_f):
    S = q.shape[1]
    scale = FOX_HEAD_DIM ** -0.5
    F = jnp.cumsum(log_f, axis=1).transpose(0, 2, 1)
    outs = []
    for start in range(0, S, Q_BLOCK):
        end = start + Q_BLOCK
        s = jnp.einsum('bqhd,bkhd->bhqk', q[:, start:end], k[:, :end]).astype(jnp.float32) * scale
        s = s + F[:, :, start:end, None] - F[:, :, None, :end]
        qpos = jnp.arange(start, end)
        kpos = jnp.arange(end)
        s = jnp.where(kpos[None, :] <= qpos[:, None], s, -jnp.inf)
        p = jax.nn.softmax(s, axis=-1)
        outs.append(jnp.einsum('bhqk,bkhd->bqhd', p.astype(v.dtype), v[:, :end]))
    return jnp.concatenate(outs, axis=1)


def pool_fox_mixer(h, w_in, b_f, pool_w, pool_scale, w_out):
    B, S, _ = h.shape
    proj = h @ w_in
    u, q, k, v, f_logit = jnp.split(
        proj, [POOL_WIDTH, POOL_WIDTH + FOX_WIDTH, POOL_WIDTH + 2 * FOX_WIDTH, POOL_WIDTH + 3 * FOX_WIDTH], axis=-1)
    pool_out = multiscale_pool(u, pool_w, pool_scale)
    log_f = jax.nn.log_sigmoid((f_logit + b_f).astype(jnp.float32))
    hs = (B, S, FOX_HEADS, FOX_HEAD_DIM)
    attn = forgetting_attention(q.reshape(hs), k.reshape(hs), v.reshape(hs), log_f)
    y = jnp.concatenate([pool_out, attn.reshape(B, S, FOX_WIDTH)], axis=-1)
    return y @ w_out


def causal_depthwise_conv(x, w):
    K, C = w.shape
    return lax.conv_general_dilated(
        x, w[:, None, :].astype(x.dtype), window_strides=(1,), padding=[(K - 1, 0)],
        dimension_numbers=('NWC', 'WIO', 'NWC'), feature_group_count=C)


def chunk_gated_delta_rule(q, k, v, g, beta):
    B, S, H, Dk = q.shape
    Dv = v.shape[-1]
    n = S // CHUNK

    def chunks(t):
        return t.reshape(B, n, CHUNK, H, -1).transpose(0, 3, 1, 2, 4)

    q = chunks(q) * (Dk ** -0.5)
    k = chunks(k)
    v = chunks(v)
    g = jnp.cumsum(chunks(g[..., None])[..., 0], axis=-1)
    beta = chunks(beta[..., None])
    k_beta = k * beta
    v_beta = v * beta
    idx = jnp.arange(CHUNK)
    causal = idx[:, None] >= idx[None, :]
    strict = idx[:, None] > idx[None, :]
    decay = jnp.exp(jnp.where(causal, g[..., :, None] - g[..., None, :], -jnp.inf))
    a = jnp.where(strict, jnp.einsum('bhnid,bhnjd->bhnij', k_beta, k) * decay, 0.0)
    eye = jnp.eye(CHUNK, dtype=a.dtype)
    t_inv = lax.linalg.triangular_solve(
        eye + a, jnp.broadcast_to(eye, a.shape), left_side=True, lower=True, unit_diagonal=True)
    u = jnp.einsum('bhnij,bhnjd->bhnid', t_inv, v_beta)
    w = jnp.einsum('bhnij,bhnjd->bhnid', t_inv, k_beta * jnp.exp(g)[..., None])
    attn = jnp.einsum('bhnid,bhnjd->bhnij', q, k) * decay

    def step(state, inp):
        q_i, k_i, u_i, w_i, g_i, attn_i = inp
        v_new = u_i - jnp.einsum('bhcd,bhde->bhce', w_i, state)
        o_i = (jnp.einsum('bhcd,bhde->bhce', q_i * jnp.exp(g_i)[..., None], state)
               + jnp.einsum('bhij,bhje->bhie', attn_i, v_new))
        g_last = g_i[..., -1]
        state = (state * jnp.exp(g_last)[..., None, None]
                 + jnp.einsum('bhcd,bhce->bhde', k_i * jnp.exp(g_last[..., None] - g_i)[..., None], v_new))
        return state, o_i

    mv = lambda t: jnp.moveaxis(t, 2, 0)
    state0 = jnp.zeros((B, H, Dk, Dv), jnp.float32)
    _, o = lax.scan(step, state0, (mv(q), mv(k), mv(u), mv(w), mv(g), mv(attn)))
    return o.transpose(1, 0, 3, 2, 4).reshape(B, S, H, Dv)


def gated_deltanet_mixer(h, w_in, conv_w, a_log, dt_bias, norm_w, w_out):
    B, S, _ = h.shape
    proj = h @ w_in
    qkv, z, b, a = jnp.split(
        proj, [GDN_CONV_CH, GDN_CONV_CH + GDN_V_DIM, GDN_CONV_CH + GDN_V_DIM + GDN_V_HEADS], axis=-1)
    qkv = jax.nn.silu(causal_depthwise_conv(qkv, conv_w)).astype(jnp.float32)
    q, k, v = jnp.split(qkv, [GDN_QK_DIM, 2 * GDN_QK_DIM], axis=-1)
    rep = GDN_V_HEADS // GDN_QK_HEADS
    q = jnp.repeat(l2norm(q.reshape(B, S, GDN_QK_HEADS, GDN_HEAD_DIM)), rep, axis=2)
    k = jnp.repeat(l2norm(k.reshape(B, S, GDN_QK_HEADS, GDN_HEAD_DIM)), rep, axis=2)
    v = v.reshape(B, S, GDN_V_HEADS, GDN_HEAD_DIM)
    beta = jax.nn.sigmoid(b.astype(jnp.float32))
    g = -jnp.exp(a_log.astype(jnp.float32)) * jax.nn.softplus(a.astype(jnp.float32) + dt_bias.astype(jnp.float32))
    o = chunk_gated_delta_rule(q, k, v, g, beta)
    o = rms_norm(o, norm_w) * jax.nn.silu(z.reshape(B, S, GDN_V_HEADS, GDN_HEAD_DIM).astype(jnp.float32))
    return o.reshape(B, S, GDN_V_DIM).astype(h.dtype) @ w_out


def setup_inputs(seed: int = 0) -> dict:
    key = jax.random.key(seed)
    ks = jax.random.split(key, 20)
    f32 = jnp.float32

    def nrm(k, shape, scale):
        return jax.random.normal(k, shape, f32) * scale

    x = nrm(ks[0], (BATCH, SEQ, D_MODEL), 1.0)
    c = nrm(ks[1], (BATCH, D_MODEL), 1.0)
    ada_w = nrm(ks[2], (DEPTH, D_MODEL, N_MOD * D_MODEL), 0.5 * D_MODEL ** -0.5)
    ada_b = nrm(ks[3], (DEPTH, N_MOD * D_MODEL), 0.01)
    norm_w = 1.0 + nrm(ks[4], (DEPTH, 3, D_MODEL), 0.02)
    ffn_w_gu = nrm(ks[5], (DEPTH, 2, D_MODEL, 2 * D_FF), D_MODEL ** -0.5)
    ffn_w_down = nrm(ks[6], (DEPTH, 2, D_FF, D_MODEL), D_FF ** -0.5)
    ab_w_in = nrm(ks[7], (N_EVEN, D_MODEL, AB_IN), D_MODEL ** -0.5)
    ab_b_f = 3.0 + nrm(ks[8], (N_EVEN, FOX_HEADS), 0.1)
    pool_w = nrm(ks[9], (N_EVEN, POOL_GROUPS, POOL_GROUP_WIDTH, POOL_GROUP_WIDTH), POOL_GROUP_WIDTH ** -0.5)
    pool_scale = 1.0 + nrm(ks[10], (N_EVEN, POOL_WIDTH), 0.02)
    ab_w_out = nrm(ks[11], (N_EVEN, AB_OUT, D_MODEL), AB_OUT ** -0.5)
    gdn_w_in = nrm(ks[12], (N_ODD, D_MODEL, GDN_IN), D_MODEL ** -0.5)
    gdn_conv_w = nrm(ks[13], (N_ODD, CONV_K, GDN_CONV_CH), CONV_K ** -0.5)
    gdn_a_log = jnp.log(jax.random.uniform(ks[14], (N_ODD, GDN_V_HEADS), f32, 1.0, 16.0))
    dt = jnp.exp(jax.random.uniform(ks[15], (N_ODD, GDN_V_HEADS), f32, math.log(1e-3), math.log(1e-1)))
    gdn_dt_bias = dt + jnp.log(-jnp.expm1(-dt))
    gdn_norm_w = 1.0 + nrm(ks[16], (N_ODD, GDN_HEAD_DIM), 0.02)
    gdn_w_out = nrm(ks[17], (N_ODD, GDN_V_DIM, D_MODEL), GDN_V_DIM ** -0.5)
    final_norm_w = 1.0 + nrm(ks[18], (D_MODEL,), 0.02)
    return {
        'x': x, 'c': c, 'ada_w': ada_w, 'ada_b': ada_b, 'norm_w': norm_w,
        'ffn_w_gu': ffn_w_gu, 'ffn_w_down': ffn_w_down,
        'ab_w_in': ab_w_in, 'ab_b_f': ab_b_f, 'pool_w': pool_w, 'pool_scale': pool_scale, 'ab_w_out': ab_w_out,
        'gdn_w_in': gdn_w_in, 'gdn_conv_w': gdn_conv_w, 'gdn_a_log': gdn_a_log, 'gdn_dt_bias': gdn_dt_bias,
        'gdn_norm_w': gdn_norm_w, 'gdn_w_out': gdn_w_out, 'final_norm_w': final_norm_w,
    }


def reference(x, c, ada_w, ada_b, norm_w, ffn_w_gu, ffn_w_down,
              ab_w_in, ab_b_f, pool_w, pool_scale, ab_w_out,
              gdn_w_in, gdn_conv_w, gdn_a_log, gdn_dt_bias, gdn_norm_w, gdn_w_out,
              final_norm_w):
    c_act = jax.nn.silu(c)
    for layer in range(DEPTH):
        mod = c_act @ ada_w[layer] + ada_b[layer]
        sh1, sc1, gt1, sh2, sc2, gt2, sh3, sc3, gt3 = jnp.split(mod, N_MOD, axis=-1)
        h = modulate(rms_norm(x, norm_w[layer, 0]), sh1, sc1)
        x = x + 0.5 * gt1[:, None, :] * swiglu(h, ffn_w_gu[layer, 0], ffn_w_down[layer, 0])
        h = modulate(rms_norm(x, norm_w[layer, 1]), sh2, sc2)
        e = layer // 2
        if layer % 2 == 0:
            y = pool_fox_mixer(h, ab_w_in[e], ab_b_f[e], pool_w[e], pool_scale[e], ab_w_out[e])
        else:
            y = gated_deltanet_mixer(h, gdn_w_in[e], gdn_conv_w[e], gdn_a_log[e], gdn_dt_bias[e],
                                     gdn_norm_w[e], gdn_w_out[e])
        x = x + gt2[:, None, :] * y
        h = modulate(rms_norm(x, norm_w[layer, 2]), sh3, sc3)
        x = x + 0.5 * gt3[:, None, :] * swiglu(h, ffn_w_gu[layer, 1], ffn_w_down[layer, 1])
    return rms_norm(x, final_norm_w)
```

```python
import functools

import jax
import jax.numpy as jnp
from jax import lax
from jax.experimental import pallas as pl
from jax.experimental.pallas import tpu as pltpu

F32 = jnp.float32
BF16 = jnp.bfloat16
EPS = 1e-6
HIGHEST = lax.Precision.HIGHEST

LANES = 128
N_MOD = 9
POOL_WINDOWS = (2, 4, 8, 16)
POOL_HALO = 16
HEAD_DIM = 128
CONV_K = 4
CONV_HALO = 8
CHUNK = 64
GDN_GROUP = 8
NEG_BIG = -1e30
VMEM_LIMIT = 56 * 1024 * 1024


def _silu(x):
    return x * jax.nn.sigmoid(x)


def _softplus(x):
    return jnp.maximum(x, 0.0) + jnp.log1p(jnp.exp(-jnp.abs(x)))


def _bdot(a, b):
    return jnp.dot(a, b, preferred_element_type=F32)


def _norm_mod(x, nw, shift, scale):
    y = x * lax.rsqrt(jnp.mean(x * x, axis=-1, keepdims=True) + EPS) * nw
    return y * (1.0 + scale) + shift


def _params(sem, vmem=VMEM_LIMIT):
    return pltpu.CompilerParams(dimension_semantics=sem, vmem_limit_bytes=vmem)


def _adaln_kernel(c_ref, w_ref, b_ref, o_ref):
    ca = _silu(c_ref[...]).astype(BF16)
    o_ref[0] = _bdot(ca, w_ref[0].astype(BF16)) + b_ref[0]


def _adaln(c, ada_w, ada_b):
    depth, d, n = ada_w.shape
    b = c.shape[0]
    tn = 1024
    return pl.pallas_call(
        _adaln_kernel,
        grid=(depth, n // tn),
        in_specs=[
            pl.BlockSpec((b, d), lambda l, j: (0, 0)),
            pl.BlockSpec((1, d, tn), lambda l, j: (l, 0, j)),
            pl.BlockSpec((1, 1, tn), lambda l, j: (l, 0, j)),
        ],
        out_specs=pl.BlockSpec((1, b, tn), lambda l, j: (l, 0, j)),
        out_shape=jax.ShapeDtypeStruct((depth, b, n), F32),
        compiler_params=_params(("parallel", "parallel")),
        name="adaln",
    )(c, ada_w, ada_b.reshape(depth, 1, n))


def _ffn_kernel(x_ref, mod_ref, nw_ref, wg_ref, wu_ref, wd_ref, fw_ref, o_ref, h_ref, *, k0, n_ff, final):
    j = pl.program_id(1)

    @pl.when(j == 0)
    def _():
        h = _norm_mod(x_ref[...], nw_ref[...], mod_ref[0, k0:k0 + 1, :], mod_ref[0, k0 + 1:k0 + 2, :])
        h_ref[...] = h.astype(BF16)

    h = h_ref[...]
    g = _bdot(h, wg_ref[...])
    u = _bdot(h, wu_ref[...])
    d = _bdot((_silu(g) * u).astype(BF16), wd_ref[...])

    @pl.when(j == 0)
    def _():
        o_ref[...] = d

    @pl.when(j > 0)
    def _():
        o_ref[...] += d

    @pl.when(j == n_ff - 1)
    def _():
        xn = x_ref[...] + 0.5 * mod_ref[0, k0 + 2:k0 + 3, :] * o_ref[...]
        if final:
            xn = xn * lax.rsqrt(jnp.mean(xn * xn, axis=-1, keepdims=True) + EPS) * fw_ref[...]
        o_ref[...] = xn


def _ffn(x, mod, nw, w_gu, w_down, fw, *, k0, seq, final, tm=512, tf=512):
    t, d = x.shape
    dff = w_down.shape[0]
    n_ff = dff // tf
    tpb = seq // tm
    kern = functools.partial(_ffn_kernel, k0=k0, n_ff=n_ff, final=final)
    return pl.pallas_call(
        kern,
        grid=(t // tm, n_ff),
        in_specs=[
            pl.BlockSpec((tm, d), lambda i, j: (i, 0)),
            pl.BlockSpec((1, N_MOD, d), lambda i, j: (i // tpb, 0, 0)),
            pl.BlockSpec((1, d), lambda i, j: (0, 0)),
            pl.BlockSpec((d, tf), lambda i, j: (0, j)),
            pl.BlockSpec((d, tf), lambda i, j: (0, j + n_ff)),
            pl.BlockSpec((tf, d), lambda i, j: (j, 0)),
            pl.BlockSpec((1, d), lambda i, j: (0, 0)),
        ],
        out_specs=pl.BlockSpec((tm, d), lambda i, j: (i, 0)),
        out_shape=jax.ShapeDtypeStruct((t, d), F32),
        scratch_shapes=[pltpu.VMEM((tm, d), BF16)],
        compiler_params=_params(("parallel", "arbitrary")),
        name="ffn",
    )(x, mod, nw, w_gu, w_gu, w_down, fw)


def _abin_kernel(x_ref, mod_ref, nw_ref, w_ref, wf_ref, bf_ref, pw_ref, ps_ref,
                 pool_ref, qkv_ref, cum_ref, h_ref, ubuf, fcarry, *, k0, tm, tpb, gw):
    i = pl.program_id(0)
    n = pl.program_id(1)
    first = (i % tpb) == 0

    @pl.when(n == 0)
    def _():
        h = _norm_mod(x_ref[...], nw_ref[...], mod_ref[0, k0:k0 + 1, :], mod_ref[0, k0 + 1:k0 + 2, :])
        h_ref[...] = h.astype(BF16)
        hb = h_ref[...]

        f = _bdot(hb, wf_ref[...]) + bf_ref[...]
        log_f = jnp.minimum(f, 0.0) - jnp.log1p(jnp.exp(-jnp.abs(f)))
        r = lax.broadcasted_iota(jnp.int32, (tm, tm), 0)
        c = lax.broadcasted_iota(jnp.int32, (tm, tm), 1)
        tri = (r >= c).astype(F32)

        @pl.when(first)
        def _():
            fcarry[...] = jnp.zeros_like(fcarry)
            ubuf[0:POOL_HALO, :] = jnp.zeros((POOL_HALO, ubuf.shape[1]), F32)

        cum = jnp.dot(tri, log_f, precision=HIGHEST, preferred_element_type=F32) + fcarry[...]
        cum_ref[...] = cum
        fcarry[...] = cum[tm - 1:tm, :]

        u = _bdot(hb, w_ref[...])
        ubuf[POOL_HALO:POOL_HALO + tm, :] = u
        pos = (i % tpb) * tm + lax.broadcasted_iota(jnp.int32, (tm, 1), 0)
        for g, win in enumerate(POOL_WINDOWS):
            sl = slice(g * gw, (g + 1) * gw)
            acc = u[:, sl]
            for s in range(1, win):
                acc = acc + ubuf[POOL_HALO - s:POOL_HALO - s + tm, sl]
            cnt = jnp.minimum(pos + 1, win).astype(F32)
            dlt = acc / cnt - u[:, sl]
            y = _bdot(dlt.astype(BF16), pw_ref[g]) * ps_ref[:, sl]
            pool_ref[:, sl] = y.astype(BF16)
        ubuf[0:POOL_HALO, :] = ubuf[tm:tm + POOL_HALO, :]

    @pl.when(n > 0)
    def _():
        qkv_ref[...] = _bdot(h_ref[...], w_ref[...]).astype(BF16)


def _abin(x, mod, nw, w_main, w_f, b_f, pool_w, pool_scale, *, k0, seq, tm=512):
    t, d = x.shape
    tn = w_main.shape[1] // 4
    tpb = seq // tm
    groups = pool_w.shape[0]
    gw = tn // groups
    kern = functools.partial(_abin_kernel, k0=k0, tm=tm, tpb=tpb, gw=gw)
    return pl.pallas_call(
        kern,
        grid=(t // tm, 4),
        in_specs=[
            pl.BlockSpec((tm, d), lambda i, n: (i, 0)),
            pl.BlockSpec((1, N_MOD, d), lambda i, n: (i // tpb, 0, 0)),
            pl.BlockSpec((1, d), lambda i, n: (0, 0)),
            pl.BlockSpec((d, tn), lambda i, n: (0, n)),
            pl.BlockSpec((d, LANES), lambda i, n: (0, 0)),
            pl.BlockSpec((1, LANES), lambda i, n: (0, 0)),
            pl.BlockSpec((groups, gw, gw), lambda i, n: (0, 0, 0)),
            pl.BlockSpec((1, tn), lambda i, n: (0, 0)),
        ],
        out_specs=[
            pl.BlockSpec((tm, tn), lambda i, n: (i, 0)),
            pl.BlockSpec((tm, tn), lambda i, n: (i, jnp.maximum(n - 1, 0))),
            pl.BlockSpec((tm, LANES), lambda i, n: (i, 0)),
        ],
        out_shape=[
            jax.ShapeDtypeStruct((t, tn), BF16),
            jax.ShapeDtypeStruct((t, 3 * tn), BF16),
            jax.ShapeDtypeStruct((t, LANES), F32),
        ],
        scratch_shapes=[
            pltpu.VMEM((tm, d), BF16),
            pltpu.VMEM((POOL_HALO + tm, tn), F32),
            pltpu.VMEM((1, LANES), F32),
        ],
        compiler_params=_params(("arbitrary", "arbitrary")),
        name="abin",
    )(x, mod, nw, w_main, w_f, b_f, pool_w, pool_scale)


def _fox_kernel(q_ref, k_ref, v_ref, fq_ref, fk_ref, o_ref, *, tq, scale):
    qi = pl.program_id(2)
    q = q_ref[0]
    fq = fq_ref[0, 0]

    def step(j, carry, masked):
        m, l, acc = carry
        off = pl.multiple_of(j * tq, tq)
        kj = k_ref[0, pl.ds(off, tq), :]
        vj = v_ref[0, pl.ds(off, tq), :]
        s = lax.dot_general(q, kj, (((1,), (1,)), ((), ())), preferred_element_type=F32)
        s = s * scale + (fq - fk_ref[0, j])
        if masked:
            r = lax.broadcasted_iota(jnp.int32, (tq, tq), 0)
            c = lax.broadcasted_iota(jnp.int32, (tq, tq), 1)
            s = jnp.where(c <= r, s, NEG_BIG)
        m_new = jnp.maximum(m, jnp.max(s, axis=-1, keepdims=True))
        alpha = jnp.exp(m - m_new)
        p = jnp.exp(s - m_new)
        l = alpha * l + jnp.sum(p, axis=-1, keepdims=True)
        acc = alpha * acc + _bdot(p.astype(BF16), vj)
        return m_new, l, acc

    init = (jnp.full((tq, 1), NEG_BIG, F32), jnp.zeros((tq, 1), F32), jnp.zeros((tq, HEAD_DIM), F32))
    carry = lax.fori_loop(0, qi, lambda j, cr: step(j, cr, False), init)
    _, l, acc = step(qi, carry, True)
    o_ref[0] = (acc / l).astype(BF16)


def _fox(qkv, cum_col, cum_row, *, batch, seq, heads, tq=512):
    qkv3 = qkv.reshape(batch, seq, 3 * heads * HEAD_DIM)
    nq = seq // tq
    kern = functools.partial(_fox_kernel, tq=tq, scale=HEAD_DIM ** -0.5)
    return pl.pallas_call(
        kern,
        grid=(batch, heads, nq),
        in_specs=[
            pl.BlockSpec((1, tq, HEAD_DIM), lambda b, h, i: (b, i, h)),
            pl.BlockSpec((1, seq, HEAD_DIM), lambda b, h, i: (b, 0, heads + h)),
            pl.BlockSpec((1, seq, HEAD_DIM), lambda b, h, i: (b, 0, 2 * heads + h)),
            pl.BlockSpec((1, 1, tq, 1), lambda b, h, i: (b, h, i, 0)),
            pl.BlockSpec((1, nq, 1, tq), lambda b, h, i: (b * heads + h, 0, 0, 0)),
        ],
        out_specs=pl.BlockSpec((1, tq, HEAD_DIM), lambda b, h, i: (b, i, h)),
        out_shape=jax.ShapeDtypeStruct((batch, seq, heads * HEAD_DIM), BF16),
        compiler_params=_params(("parallel", "parallel", "arbitrary")),
        name="fox",
    )(qkv3, qkv3, qkv3, cum_col, cum_row)


def _proj_res_kernel(*refs, n_in, gate_row):
    x_ref, mod_ref = refs[0], refs[1]
    a_refs = refs[2:2 + n_in]
    w_refs = refs[2 + n_in:2 + 2 * n_in]
    o_ref = refs[-1]
    y = _bdot(a_refs[0][...], w_refs[0][...])
    for a_ref, w_ref in zip(a_refs[1:], w_refs[1:]):
        y = y + _bdot(a_ref[...], w_ref[...])
    o_ref[...] = x_ref[...] + mod_ref[0, gate_row:gate_row + 1, :] * y


def _proj_res(x, mod, acts, weights, *, gate_row, seq, tm=256):
    t, d = x.shape
    tpb = seq // tm
    n_in = len(acts)
    kern = functools.partial(_proj_res_kernel, n_in=n_in, gate_row=gate_row)
    in_specs = [
        pl.BlockSpec((tm, d), lambda i: (i, 0)),
        pl.BlockSpec((1, N_MOD, d), lambda i: (i // tpb, 0, 0)),
    ]
    in_specs += [pl.BlockSpec((tm, a.shape[1]), lambda i: (i, 0)) for a in acts]
    in_specs += [pl.BlockSpec(w.shape, lambda i: (0, 0), pipeline_mode=pl.Buffered(1)) for w in weights]
    return pl.pallas_call(
        kern,
        grid=(t // tm,),
        in_specs=in_specs,
        out_specs=pl.BlockSpec((tm, d), lambda i: (i, 0)),
        out_shape=jax.ShapeDtypeStruct((t, d), F32),
        compiler_params=_params(("parallel",)),
        name="proj_res",
    )(x, mod, *acts, *weights)


def _gdnin_kernel(x_ref, mod_ref, nw_ref, w_ref, wg_ref, alog_ref, dtb_ref, cw_ref,
                  o_ref, gate_ref, h_ref, pbuf, carry_ref, *, k0, tm, tpb, n_conv, n_qk, n_q):
    i = pl.program_id(0)
    n = pl.program_id(1)
    first = (i % tpb) == 0

    @pl.when(n == 0)
    def _():
        h = _norm_mod(x_ref[...], nw_ref[...], mod_ref[0, k0:k0 + 1, :], mod_ref[0, k0 + 1:k0 + 2, :])
        h_ref[...] = h.astype(BF16)
        ba = _bdot(h_ref[...], wg_ref[...])
        lane = lax.broadcasted_iota(jnp.int32, ba.shape, 1) % LANES
        g = -jnp.exp(alog_ref[...]) * _softplus(ba + dtb_ref[...])
        beta = jax.nn.sigmoid(ba)
        r = lax.broadcasted_iota(jnp.int32, (tm, tm), 0)
        c = lax.broadcasted_iota(jnp.int32, (tm, tm), 1)
        tri = ((r >= c) & ((r // CHUNK) == (c // CHUNK))).astype(F32)
        gcum = jnp.dot(tri, g, precision=HIGHEST, preferred_element_type=F32)
        gate_ref[...] = jnp.where(lane < GDN_GROUP, gcum, beta)

    acc = _bdot(h_ref[...], w_ref[...])

    @pl.when(n < n_conv)
    def _():
        @pl.when(first)
        def _():
            carry_ref[n] = jnp.zeros(carry_ref.shape[1:], F32)

        pbuf[0:CONV_HALO, :] = carry_ref[n]
        pbuf[CONV_HALO:CONV_HALO + tm, :] = acc
        y = cw_ref[CONV_K - 1:CONV_K, :] * acc
        for kk in range(CONV_K - 1):
            lo = CONV_HALO - (CONV_K - 1) + kk
            y = y + cw_ref[kk:kk + 1, :] * pbuf[lo:lo + tm, :]
        carry_ref[n] = pbuf[tm:tm + CONV_HALO, :]
        y = _silu(y)

        @pl.when(n < n_qk)
        def _():
            qscale = jnp.where(n < n_q, HEAD_DIM ** -0.5, 1.0).astype(F32)
            for hh in range(y.shape[1] // HEAD_DIM):
                seg = y[:, hh * HEAD_DIM:(hh + 1) * HEAD_DIM]
                inv = lax.rsqrt(jnp.sum(seg * seg, axis=-1, keepdims=True) + EPS) * qscale
                o_ref[:, hh * HEAD_DIM:(hh + 1) * HEAD_DIM] = (seg * inv).astype(BF16)

        @pl.when(n >= n_qk)
        def _():
            o_ref[...] = y.astype(BF16)

    @pl.when(n >= n_conv)
    def _():
        o_ref[...] = acc.astype(BF16)


def _gdnin(x, mod, nw, w_main, w_gate, alog_l, dtb_l, conv_w, *, k0, seq, qk_dim, conv_ch, tm=512, tn=1024):
    t, d = x.shape
    n_total = w_main.shape[1]
    tpb = seq // tm
    n_conv = conv_ch // tn
    n_qk = 2 * qk_dim // tn
    n_q = qk_dim // tn
    gl = w_gate.shape[1]
    kern = functools.partial(_gdnin_kernel, k0=k0, tm=tm, tpb=tpb, n_conv=n_conv, n_qk=n_qk, n_q=n_q)
    return pl.pallas_call(
        kern,
        grid=(t // tm, n_total // tn),
        in_specs=[
            pl.BlockSpec((tm, d), lambda i, n: (i, 0)),
            pl.BlockSpec((1, N_MOD, d), lambda i, n: (i // tpb, 0, 0)),
            pl.BlockSpec((1, d), lambda i, n: (0, 0)),
            pl.BlockSpec((d, tn), lambda i, n: (0, n)),
            pl.BlockSpec((d, gl), lambda i, n: (0, 0)),
            pl.BlockSpec((1, gl), lambda i, n: (0, 0)),
            pl.BlockSpec((1, gl), lambda i, n: (0, 0)),
            pl.BlockSpec((CONV_K, tn), lambda i, n: (0, jnp.minimum(n, n_conv - 1))),
        ],
        out_specs=[
            pl.BlockSpec((tm, tn), lambda i, n: (i, n)),
            pl.BlockSpec((tm, gl), lambda i, n: (i, 0)),
        ],
        out_shape=[
            jax.ShapeDtypeStruct((t, n_total), BF16),
            jax.ShapeDtypeStruct((t, gl), F32),
        ],
        scratch_shapes=[
            pltpu.VMEM((tm, d), BF16),
            pltpu.VMEM((CONV_HALO + tm, tn), F32),
            pltpu.VMEM((n_conv, CONV_HALO, tn), F32),
        ],
        compiler_params=_params(("arbitrary", "arbitrary")),
        name="gdnin",
    )(x, mod, nw, w_main, w_gate, alog_l, dtb_l, conv_w)


def _gdn_kernel(q_ref, k_ref, v_ref, z_ref, gt_ref, nw_ref, o_ref, state_ref):
    @pl.when(pl.program_id(2) == 0)
    def _():
        state_ref[...] = jnp.zeros_like(state_ref)

    c = CHUNK
    hd = HEAD_DIM
    gates = gt_ref[0]
    row = lax.broadcasted_iota(jnp.int32, (c, 2 * c), 0)
    lane = lax.broadcasted_iota(jnp.int32, (c, 2 * c), 1)
    col = lane % c
    left = lane < c
    tril = row >= col
    strict = row > col
    eye2 = (row == col).astype(F32)
    br = lax.broadcasted_iota(jnp.int32, (2 * c, 2 * c), 0) // c
    bc = lax.broadcasted_iota(jnp.int32, (2 * c, 2 * c), 1) // c
    same_block = br == bc

    def blockdiag(p):
        return jnp.where(same_block, jnp.concatenate([p, p], axis=0), 0.0).astype(BF16)

    gates_t = jnp.concatenate([gates, gates], axis=0).T
    exp_g = jnp.exp(gates)
    g_last = gates[c - 1:c, :]
    exp_rest = jnp.exp(g_last - gates)
    exp_last = jnp.exp(g_last)
    nw = nw_ref[...]
    zeros_b = jnp.zeros((c, hd), BF16)

    for jq in range(GDN_GROUP // 2):
        ha, hb = 2 * jq, 2 * jq + 1
        q = q_ref[0, :, jq * hd:(jq + 1) * hd]
        k = k_ref[0, :, jq * hd:(jq + 1) * hd]
        qf = q.astype(F32)
        kf = k.astype(F32)
        kq = lax.dot_general(jnp.concatenate([k, q], axis=0), jnp.concatenate([k, k], axis=0),
                             (((1,), (1,)), ((), ())), preferred_element_type=F32)
        g_col = jnp.where(left, gates[:, ha:ha + 1], gates[:, hb:hb + 1])
        g_row = jnp.where(left[0:1], gates_t[ha:ha + 1, :], gates_t[hb:hb + 1, :])
        beta_a = gates[:, GDN_GROUP + ha:GDN_GROUP + ha + 1]
        beta_b = gates[:, GDN_GROUP + hb:GDN_GROUP + hb + 1]
        decay = jnp.where(tril, jnp.exp(g_col - g_row), 0.0)
        a_mat = jnp.where(strict, jnp.where(left, beta_a, beta_b) * kq[:c] * decay, 0.0)

        p = -a_mat
        x = eye2 + p
        for _ in range(5):
            p = _bdot(p.astype(BF16), blockdiag(p))
            x = x + _bdot(x.astype(BF16), blockdiag(p))
        t_inv = x.astype(BF16)

        heads = ((ha, beta_a), (hb, beta_b))
        vbs, kbs = [], []
        for h, beta in heads:
            vf = v_ref[0, :, h * hd:(h + 1) * hd].astype(F32)
            vbs.append((vf * beta).astype(BF16))
            kbs.append((kf * (beta * exp_g[:, h:h + 1])).astype(BF16))
        rhs = jnp.concatenate([
            jnp.concatenate([vbs[0], zeros_b, kbs[0], zeros_b], axis=1),
            jnp.concatenate([zeros_b, vbs[1], zeros_b, kbs[1]], axis=1)], axis=0)
        uw = _bdot(t_inv, rhs)

        v_new, o_inter, states = [], [], []
        for r, (h, _) in enumerate(heads):
            s = state_ref[h]
            states.append(s)
            qg = (qf * exp_g[:, h:h + 1]).astype(BF16)
            w_r = uw[:, (2 + r) * hd:(3 + r) * hd].astype(BF16)
            wq = _bdot(jnp.concatenate([w_r, qg], axis=0), s.astype(BF16))
            v_new.append((uw[:, r * hd:(r + 1) * hd] - wq[:c]).astype(BF16))
            o_inter.append(wq[c:])
        vn = jnp.concatenate([
            jnp.concatenate([v_new[0], zeros_b], axis=1),
            jnp.concatenate([zeros_b, v_new[1]], axis=1)], axis=0)
        o_intra = _bdot((kq[c:] * decay).astype(BF16), vn)
        kg = jnp.concatenate([(kf * exp_rest[:, h:h + 1]).astype(BF16) for h, _ in heads], axis=0)
        ds = lax.dot_general(kg, vn, (((0,), (0,)), ((), ())), preferred_element_type=F32)

        for r, (h, _) in enumerate(heads):
            state_ref[h] = states[r] * exp_last[0:1, h:h + 1] + ds[:, r * hd:(r + 1) * hd]
            o = o_inter[r] + o_intra[:, r * hd:(r + 1) * hd]
            on = o * lax.rsqrt(jnp.mean(o * o, axis=-1, keepdims=True) + EPS) * nw
            zf = z_ref[0, :, h * hd:(h + 1) * hd].astype(F32)
            o_ref[0, :, h * hd:(h + 1) * hd] = (on * _silu(zf)).astype(BF16)


def _gdn(qkvz, gates, norm_w, *, batch, seq, qk_dim, v_dim):
    total = qkvz.shape[1]
    x3 = qkvz.reshape(batch, seq, total)
    g3 = gates.reshape(batch, seq, gates.shape[1])
    hd = HEAD_DIM
    qw = GDN_GROUP // 2 * hd
    vw = GDN_GROUP * hd
    n_groups = v_dim // vw
    k_off = qk_dim // qw
    v_off = 2 * qk_dim // vw
    z_off = (2 * qk_dim + v_dim) // vw
    return pl.pallas_call(
        _gdn_kernel,
        grid=(batch, n_groups, seq // CHUNK),
        in_specs=[
            pl.BlockSpec((1, CHUNK, qw), lambda b, g, c: (b, c, g)),
            pl.BlockSpec((1, CHUNK, qw), lambda b, g, c: (b, c, k_off + g)),
            pl.BlockSpec((1, CHUNK, vw), lambda b, g, c: (b, c, v_off + g)),
            pl.BlockSpec((1, CHUNK, vw), lambda b, g, c: (b, c, z_off + g)),
            pl.BlockSpec((1, CHUNK, LANES), lambda b, g, c: (b, c, g)),
            pl.BlockSpec((1, hd), lambda b, g, c: (0, 0)),
        ],
        out_specs=pl.BlockSpec((1, CHUNK, vw), lambda b, g, c: (b, c, g)),
        out_shape=jax.ShapeDtypeStruct((batch, seq, v_dim), BF16),
        scratch_shapes=[pltpu.VMEM((GDN_GROUP, hd, hd), F32)],
        compiler_params=_params(("parallel", "parallel", "arbitrary")),
        name="gdn",
    )(x3, x3, x3, x3, g3, norm_w)


def _grouped_gate_layout(w_b, w_a, a_log, dt_bias):
    d, hv = w_a.shape
    ng = hv // GDN_GROUP
    pad = LANES - 2 * GDN_GROUP
    wa = w_a.reshape(d, ng, GDN_GROUP)
    wb = w_b.reshape(d, ng, GDN_GROUP)
    w = jnp.concatenate([wa, wb, jnp.zeros((d, ng, pad), w_a.dtype)], axis=-1).reshape(d, ng * LANES)
    zg = jnp.zeros((ng, LANES - GDN_GROUP), F32)
    alog = jnp.concatenate([a_log.reshape(ng, GDN_GROUP).astype(F32), zg], axis=-1).reshape(1, ng * LANES)
    dtb = jnp.concatenate([dt_bias.reshape(ng, GDN_GROUP).astype(F32), zg], axis=-1).reshape(1, ng * LANES)
    return w.astype(BF16), alog, dtb


def kernel(x, c, ada_w, ada_b, norm_w, ffn_w_gu, ffn_w_down, ab_w_in, ab_b_f, pool_w, pool_scale, ab_w_out,
           gdn_w_in, gdn_conv_w, gdn_a_log, gdn_dt_bias, gdn_norm_w, gdn_w_out, final_norm_w):
    batch, seq, d = x.shape
    depth = ada_w.shape[0]
    t = batch * seq

    mod_all = _adaln(c, ada_w, ada_b).reshape(depth, batch, N_MOD, d)
    w_gu = ffn_w_gu.astype(BF16)
    w_down = ffn_w_down.astype(BF16)
    fw = final_norm_w.reshape(1, d)
    xf = x.reshape(t, d)

    for layer in range(depth):
        mod = mod_all[layer]
        e = layer // 2
        nws = norm_w[layer].reshape(3, 1, d)
        xf = _ffn(xf, mod, nws[0], w_gu[layer, 0], w_down[layer, 0], fw, k0=0, seq=seq, final=False)

        if layer % 2 == 0:
            pool_width = pool_scale.shape[1]
            fox_heads = ab_b_f.shape[1]
            fox_width = fox_heads * HEAD_DIM
            n_main = pool_width + 3 * fox_width
            w_in = ab_w_in[e]
            w_main = w_in[:, :n_main].astype(BF16)
            w_f = jnp.pad(w_in[:, n_main:], ((0, 0), (0, LANES - fox_heads))).astype(BF16)
            b_f = jnp.pad(ab_b_f[e], (0, LANES - fox_heads)).reshape(1, LANES)
            pool_out, qkv, cum = _abin(xf, mod, nws[1], w_main, w_f, b_f, pool_w[e].astype(BF16),
                                       pool_scale[e].reshape(1, pool_width), k0=3, seq=seq)
            tq = 512
            cum_t = cum.reshape(batch, seq, LANES)[:, :, :fox_heads].transpose(0, 2, 1)
            attn = _fox(qkv, cum_t.reshape(batch, fox_heads, seq, 1),
                        cum_t.reshape(batch * fox_heads, seq // tq, 1, tq),
                        batch=batch, seq=seq, heads=fox_heads, tq=tq)
            w_out = ab_w_out[e].astype(BF16)
            xf = _proj_res(xf, mod, [pool_out, attn.reshape(t, fox_width)],
                           [w_out[:pool_width], w_out[pool_width:]], gate_row=5, seq=seq)
        else:
            v_heads = gdn_a_log.shape[1]
            v_dim = v_heads * HEAD_DIM
            conv_ch = gdn_conv_w.shape[2]
            qk_dim = (conv_ch - v_dim) // 2
            n_main = conv_ch + v_dim
            w_in = gdn_w_in[e]
            w_gate, alog_l, dtb_l = _grouped_gate_layout(
                w_in[:, n_main:n_main + v_heads], w_in[:, n_main + v_heads:], gdn_a_log[e], gdn_dt_bias[e])
            qkvz, gates = _gdnin(xf, mod, nws[1], w_in[:, :n_main].astype(BF16), w_gate, alog_l, dtb_l,
                                 gdn_conv_w[e], k0=3, seq=seq, qk_dim=qk_dim, conv_ch=conv_ch)
            o = _gdn(qkvz, gates, gdn_norm_w[e].reshape(1, HEAD_DIM), batch=batch, seq=seq,
                     qk_dim=qk_dim, v_dim=v_dim)
            xf = _proj_res(xf, mod, [o.reshape(t, v_dim)], [gdn_w_out[e].astype(BF16)], gate_row=5, seq=seq)

        xf = _ffn(xf, mod, nws[2], w_gu[layer, 1], w_down[layer, 1], fw, k0=6, seq=seq,
                  final=(layer == depth - 1))
    return xf.reshape(batch, seq, d)
```

```python
import functools

import jax
import jax.numpy as jnp
from jax import lax
from jax.experimental import pallas as pl
from jax.experimental.pallas import tpu as pltpu

F32 = jnp.float32
BF16 = jnp.bfloat16
EPS = 1e-6

LANES = 128
MXU_COLS = 256
N_MOD = 9
POOL_WINDOWS = (2, 4, 8, 16)
POOL_HALO = 16
HEAD_DIM = 128
CONV_K = 4
CONV_HALO = 8
CHUNK = 64
GDN_GROUP = 8
NEG_BIG = -1e30
VMEM_LIMIT = 56 * 1024 * 1024


def _silu(x):
    return x * jax.nn.sigmoid(x)


def _softplus(x):
    return jnp.maximum(x, 0.0) + jnp.log1p(jnp.exp(-jnp.abs(x)))


def _bdot(a, b):
    return jnp.dot(a, b, preferred_element_type=F32)


def _tri_cumsum(tri, x):
    x1 = x.astype(BF16)
    r1 = x - x1.astype(F32)
    x2 = r1.astype(BF16)
    x3 = (r1 - x2.astype(F32)).astype(BF16)
    s = _bdot(tri, jnp.concatenate([x1, x2, x3], axis=1))
    return s[:, 0:LANES] + s[:, LANES:2 * LANES] + s[:, 2 * LANES:3 * LANES]


def _norm_mod(x, nw, shift, scale):
    y = x * lax.rsqrt(jnp.mean(x * x, axis=-1, keepdims=True) + EPS) * nw
    return y * (1.0 + scale) + shift


def _params(sem, vmem=VMEM_LIMIT):
    return pltpu.CompilerParams(dimension_semantics=sem, vmem_limit_bytes=vmem)


def _adaln_kernel(c_ref, w_ref, b_ref, o_ref):
    ca = _silu(c_ref[...]).astype(BF16)
    o_ref[0] = _bdot(ca, w_ref[0].astype(BF16)) + b_ref[0]


def _adaln(c, ada_w, ada_b):
    depth, d, n = ada_w.shape
    b = c.shape[0]
    tn = 1024
    return pl.pallas_call(
        _adaln_kernel,
        grid=(depth, n // tn),
        in_specs=[
            pl.BlockSpec((b, d), lambda l, j: (0, 0)),
            pl.BlockSpec((1, d, tn), lambda l, j: (l, 0, j)),
            pl.BlockSpec((1, 1, tn), lambda l, j: (l, 0, j)),
        ],
        out_specs=pl.BlockSpec((1, b, tn), lambda l, j: (l, 0, j)),
        out_shape=jax.ShapeDtypeStruct((depth, b, n), F32),
        compiler_params=_params(("parallel", "parallel")),
        name="adaln",
    )(c, ada_w, ada_b.reshape(depth, 1, n))


def _ffn_kernel(x_ref, mod_ref, nw_ref, wg_ref, wu_ref, wd_ref, fw_ref, o_ref, h_ref, *, k0, n_ff, final):
    j = pl.program_id(1)

    @pl.when(j == 0)
    def _():
        h = _norm_mod(x_ref[...], nw_ref[...], mod_ref[0, k0:k0 + 1, :], mod_ref[0, k0 + 1:k0 + 2, :])
        h_ref[...] = h.astype(BF16)
        o_ref[...] = jnp.zeros_like(o_ref)

    h = h_ref[...]
    g = _bdot(h, wg_ref[...])
    u = _bdot(h, wu_ref[...])
    o_ref[...] += _bdot((_silu(g) * u).astype(BF16), wd_ref[...])

    @pl.when(j == n_ff - 1)
    def _():
        xn = x_ref[...] + 0.5 * mod_ref[0, k0 + 2:k0 + 3, :] * o_ref[...]
        if final:
            xn = xn * lax.rsqrt(jnp.mean(xn * xn, axis=-1, keepdims=True) + EPS) * fw_ref[...]
        o_ref[...] = xn


def _ffn(x, mod, nw, w_gu, w_down, fw, *, k0, seq, final, tm=512, tf=512):
    t, d = x.shape
    dff = w_down.shape[0]
    n_ff = dff // tf
    tpb = seq // tm
    kern = functools.partial(_ffn_kernel, k0=k0, n_ff=n_ff, final=final)
    return pl.pallas_call(
        kern,
        grid=(t // tm, n_ff),
        in_specs=[
            pl.BlockSpec((tm, d), lambda i, j: (i, 0)),
            pl.BlockSpec((1, N_MOD, d), lambda i, j: (i // tpb, 0, 0)),
            pl.BlockSpec((1, d), lambda i, j: (0, 0)),
            pl.BlockSpec((d, tf), lambda i, j: (0, j)),
            pl.BlockSpec((d, tf), lambda i, j: (0, j + n_ff)),
            pl.BlockSpec((tf, d), lambda i, j: (j, 0)),
            pl.BlockSpec((1, d), lambda i, j: (0, 0)),
        ],
        out_specs=pl.BlockSpec((tm, d), lambda i, j: (i, 0)),
        out_shape=jax.ShapeDtypeStruct((t, d), F32),
        scratch_shapes=[pltpu.VMEM((tm, d), BF16)],
        compiler_params=_params(("parallel", "arbitrary")),
        name="ffn",
    )(x, mod, nw, w_gu, w_gu, w_down, fw)


def _abin_kernel(x_ref, mod_ref, nw_ref, w_ref, wf_ref, bf_ref, pw_ref, ps_ref,
                 pool_ref, qkv_ref, cum_ref, h_ref, ubuf, fcarry, *, k0, tm, tpb, gw):
    i = pl.program_id(0)
    n = pl.program_id(1)
    first = (i % tpb) == 0

    @pl.when(n == 0)
    def _():
        h = _norm_mod(x_ref[...], nw_ref[...], mod_ref[0, k0:k0 + 1, :], mod_ref[0, k0 + 1:k0 + 2, :])
        h_ref[...] = h.astype(BF16)
        hb = h_ref[...]

        f = _bdot(hb, wf_ref[...]) + bf_ref[...]
        log_f = jnp.minimum(f, 0.0) - jnp.log1p(jnp.exp(-jnp.abs(f)))
        r = lax.broadcasted_iota(jnp.int32, (tm, tm), 0)
        c = lax.broadcasted_iota(jnp.int32, (tm, tm), 1)
        tri = jnp.where(r >= c, 1.0, 0.0).astype(BF16)

        @pl.when(first)
        def _():
            fcarry[...] = jnp.zeros_like(fcarry)
            ubuf[0:POOL_HALO, :] = jnp.zeros((POOL_HALO, ubuf.shape[1]), F32)

        cum = _tri_cumsum(tri, log_f) + fcarry[...]
        cum_ref[...] = cum
        fcarry[...] = cum[tm - 1:tm, :]

        u = _bdot(hb, w_ref[...])
        ubuf[POOL_HALO:POOL_HALO + tm, :] = u
        pos = (i % tpb) * tm + lax.broadcasted_iota(jnp.int32, (tm, 1), 0)
        for g, win in enumerate(POOL_WINDOWS):
            sl = slice(g * gw, (g + 1) * gw)
            acc = u[:, sl]
            for s in range(1, win):
                acc = acc + ubuf[POOL_HALO - s:POOL_HALO - s + tm, sl]
            cnt = jnp.minimum(pos + 1, win).astype(F32)
            dlt = acc / cnt - u[:, sl]
            y = _bdot(dlt.astype(BF16), pw_ref[g]) * ps_ref[:, sl]
            pool_ref[:, sl] = y.astype(BF16)
        ubuf[0:POOL_HALO, :] = ubuf[tm:tm + POOL_HALO, :]

    @pl.when(n > 0)
    def _():
        qkv_ref[...] = _bdot(h_ref[...], w_ref[...]).astype(BF16)


def _abin(x, mod, nw, w_main, w_f, b_f, pool_w, pool_scale, *, k0, seq, tm=512):
    t, d = x.shape
    tn = w_main.shape[1] // 4
    tpb = seq // tm
    groups = pool_w.shape[0]
    gw = tn // groups
    kern = functools.partial(_abin_kernel, k0=k0, tm=tm, tpb=tpb, gw=gw)
    return pl.pallas_call(
        kern,
        grid=(t // tm, 4),
        in_specs=[
            pl.BlockSpec((tm, d), lambda i, n: (i, 0)),
            pl.BlockSpec((1, N_MOD, d), lambda i, n: (i // tpb, 0, 0)),
            pl.BlockSpec((1, d), lambda i, n: (0, 0)),
            pl.BlockSpec((d, tn), lambda i, n: (0, n)),
            pl.BlockSpec((d, LANES), lambda i, n: (0, 0)),
            pl.BlockSpec((1, LANES), lambda i, n: (0, 0)),
            pl.BlockSpec((groups, gw, gw), lambda i, n: (0, 0, 0)),
            pl.BlockSpec((1, tn), lambda i, n: (0, 0)),
        ],
        out_specs=[
            pl.BlockSpec((tm, tn), lambda i, n: (i, 0)),
            pl.BlockSpec((tm, tn), lambda i, n: (i, jnp.maximum(n - 1, 0))),
            pl.BlockSpec((tm, LANES), lambda i, n: (i, 0)),
        ],
        out_shape=[
            jax.ShapeDtypeStruct((t, tn), BF16),
            jax.ShapeDtypeStruct((t, 3 * tn), BF16),
            jax.ShapeDtypeStruct((t, LANES), F32),
        ],
        scratch_shapes=[
            pltpu.VMEM((tm, d), BF16),
            pltpu.VMEM((POOL_HALO + tm, tn), F32),
            pltpu.VMEM((1, LANES), F32),
        ],
        compiler_params=_params(("arbitrary", "arbitrary")),
        name="abin",
    )(x, mod, nw, w_main, w_f, b_f, pool_w, pool_scale)


def _fox_kernel(q_ref, k_ref, v_ref, fq_ref, fk_ref, o_ref, *, tq, scale):
    qi = pl.program_id(2)
    q = q_ref[0]
    fq = fq_ref[0, 0]

    def step(j, carry, masked):
        m, l, acc = carry
        off = pl.multiple_of(j * tq, tq)
        kj = k_ref[0, pl.ds(off, tq), :]
        vj = v_ref[0, pl.ds(off, tq), :]
        s = lax.dot_general(q, kj, (((1,), (1,)), ((), ())), preferred_element_type=F32)
        s = s * scale + (fq - fk_ref[0, j])
        if masked:
            r = lax.broadcasted_iota(jnp.int32, (tq, tq), 0)
            c = lax.broadcasted_iota(jnp.int32, (tq, tq), 1)
            s = jnp.where(c <= r, s, NEG_BIG)
        m_new = jnp.maximum(m, jnp.max(s, axis=-1, keepdims=True))
        alpha = jnp.exp(m - m_new)
        p = jnp.exp(s - m_new)
        l = alpha * l + jnp.sum(p, axis=-1, keepdims=True)
        acc = alpha * acc + _bdot(p.astype(BF16), vj)
        return m_new, l, acc

    init = (jnp.full((tq, 1), NEG_BIG, F32), jnp.zeros((tq, 1), F32), jnp.zeros((tq, HEAD_DIM), F32))
    carry = lax.fori_loop(0, qi, lambda j, cr: step(j, cr, False), init)
    _, l, acc = step(qi, carry, True)
    o_ref[0] = (acc / l).astype(BF16)


def _fox(qkv, cum_col, cum_row, *, batch, seq, heads, tq=512):
    qkv3 = qkv.reshape(batch, seq, 3 * heads * HEAD_DIM)
    nq = seq // tq
    kern = functools.partial(_fox_kernel, tq=tq, scale=HEAD_DIM ** -0.5)
    return pl.pallas_call(
        kern,
        grid=(batch, heads, nq),
        in_specs=[
            pl.BlockSpec((1, tq, HEAD_DIM), lambda b, h, i: (b, i, h)),
            pl.BlockSpec((1, seq, HEAD_DIM), lambda b, h, i: (b, 0, heads + h)),
            pl.BlockSpec((1, seq, HEAD_DIM), lambda b, h, i: (b, 0, 2 * heads + h)),
            pl.BlockSpec((1, 1, tq, 1), lambda b, h, i: (b, h, i, 0)),
            pl.BlockSpec((1, nq, 1, tq), lambda b, h, i: (b * heads + h, 0, 0, 0)),
        ],
        out_specs=pl.BlockSpec((1, tq, HEAD_DIM), lambda b, h, i: (b, i, h)),
        out_shape=jax.ShapeDtypeStruct((batch, seq, heads * HEAD_DIM), BF16),
        compiler_params=_params(("parallel", "parallel", "arbitrary")),
        name="fox",
    )(qkv3, qkv3, qkv3, cum_col, cum_row)


def _proj_res_kernel(*refs, n_in, gate_row):
    x_ref, mod_ref = refs[0], refs[1]
    a_refs = refs[2:2 + n_in]
    w_refs = refs[2 + n_in:2 + 2 * n_in]
    o_ref = refs[-1]
    y = _bdot(a_refs[0][...], w_refs[0][...])
    for a_ref, w_ref in zip(a_refs[1:], w_refs[1:]):
        y = y + _bdot(a_ref[...], w_ref[...])
    o_ref[...] = x_ref[...] + mod_ref[0, gate_row:gate_row + 1, :] * y


def _proj_res(x, mod, acts, weights, *, gate_row, seq, tm=256):
    t, d = x.shape
    tpb = seq // tm
    n_in = len(acts)
    kern = functools.partial(_proj_res_kernel, n_in=n_in, gate_row=gate_row)
    in_specs = [
        pl.BlockSpec((tm, d), lambda i: (i, 0)),
        pl.BlockSpec((1, N_MOD, d), lambda i: (i // tpb, 0, 0)),
    ]
    in_specs += [pl.BlockSpec((tm, a.shape[1]), lambda i: (i, 0)) for a in acts]
    in_specs += [pl.BlockSpec(w.shape, lambda i: (0, 0), pipeline_mode=pl.Buffered(1)) for w in weights]
    return pl.pallas_call(
        kern,
        grid=(t // tm,),
        in_specs=in_specs,
        out_specs=pl.BlockSpec((tm, d), lambda i: (i, 0)),
        out_shape=jax.ShapeDtypeStruct((t, d), F32),
        compiler_params=_params(("parallel",)),
        name="proj_res",
    )(x, mod, *acts, *weights)


def _gdnin_kernel(x_ref, mod_ref, nw_ref, w_ref, wg_ref, alog_ref, dtb_ref, cw_ref,
                  o_ref, gate_ref, h_ref, pbuf, carry_ref, *, k0, tm, tpb, n_conv, n_qk, n_q):
    i = pl.program_id(0)
    n = pl.program_id(1)
    first = (i % tpb) == 0

    @pl.when(n == 0)
    def _():
        h = _norm_mod(x_ref[...], nw_ref[...], mod_ref[0, k0:k0 + 1, :], mod_ref[0, k0 + 1:k0 + 2, :])
        h_ref[...] = h.astype(BF16)
        ba = _bdot(h_ref[...], wg_ref[...])
        lane = lax.broadcasted_iota(jnp.int32, ba.shape, 1) % LANES
        g = -jnp.exp(alog_ref[...]) * _softplus(ba + dtb_ref[...])
        gate_ref[...] = jnp.where(lane < GDN_GROUP, g, jax.nn.sigmoid(ba))

        @pl.when(i == 0)
        def _():
            carry_ref[...] = jnp.zeros_like(carry_ref)

    tn = w_ref.shape[1]
    n_sub = tn // MXU_COLS

    def matmul(hb, j):
        return _bdot(hb, w_ref[:, j * MXU_COLS:(j + 1) * MXU_COLS])

    def conv_silu(j, acc):
        cols = slice(j * MXU_COLS, (j + 1) * MXU_COLS)
        pbuf[0:CONV_HALO, cols] = jnp.where(first, 0.0, carry_ref[n, :, cols])
        pbuf[CONV_HALO:CONV_HALO + tm, cols] = acc
        carry_ref[n, :, cols] = acc[tm - CONV_HALO:tm, :]
        y = cw_ref[CONV_K - 1:CONV_K, cols] * acc
        for kk in range(CONV_K - 1):
            lo = CONV_HALO - (CONV_K - 1) + kk
            y = y + cw_ref[kk:kk + 1, cols] * pbuf[lo:lo + tm, cols]
        return _silu(y)

    def pipelined(epilogue):
        hb = h_ref[...]
        acc = matmul(hb, 0)
        for j in range(1, n_sub):
            nxt = matmul(hb, j)
            epilogue(j - 1, acc)
            acc = nxt
        epilogue(n_sub - 1, acc)

    @pl.when(n < n_qk)
    def _():
        qscale = jnp.where(n < n_q, HEAD_DIM ** -0.5, 1.0).astype(F32)

        def epilogue(j, acc):
            y = conv_silu(j, acc)
            for hh in range(MXU_COLS // HEAD_DIM):
                seg = y[:, hh * HEAD_DIM:(hh + 1) * HEAD_DIM]
                inv = lax.rsqrt(jnp.sum(seg * seg, axis=-1, keepdims=True) + EPS) * qscale
                lo = j * MXU_COLS + hh * HEAD_DIM
                o_ref[:, lo:lo + HEAD_DIM] = (seg * inv).astype(BF16)

        pipelined(epilogue)

    @pl.when((n >= n_qk) & (n < n_conv))
    def _():
        def epilogue(j, acc):
            o_ref[:, j * MXU_COLS:(j + 1) * MXU_COLS] = conv_silu(j, acc).astype(BF16)

        pipelined(epilogue)

    @pl.when(n >= n_conv)
    def _():
        o_ref[...] = _bdot(h_ref[...], w_ref[...]).astype(BF16)


def _gdnin(x, mod, nw, w_main, w_gate, alog_l, dtb_l, conv_w, *, k0, seq, qk_dim, conv_ch, tm=512, tn=1024):
    t, d = x.shape
    n_total = w_main.shape[1]
    tpb = seq // tm
    n_conv = conv_ch // tn
    n_qk = 2 * qk_dim // tn
    n_q = qk_dim // tn
    gl = w_gate.shape[1]
    kern = functools.partial(_gdnin_kernel, k0=k0, tm=tm, tpb=tpb, n_conv=n_conv, n_qk=n_qk, n_q=n_q)
    return pl.pallas_call(
        kern,
        grid=(t // tm, n_total // tn),
        in_specs=[
            pl.BlockSpec((tm, d), lambda i, n: (i, 0)),
            pl.BlockSpec((1, N_MOD, d), lambda i, n: (i // tpb, 0, 0)),
            pl.BlockSpec((1, d), lambda i, n: (0, 0)),
            pl.BlockSpec((d, tn), lambda i, n: (0, n)),
            pl.BlockSpec((d, gl), lambda i, n: (0, 0)),
            pl.BlockSpec((1, gl), lambda i, n: (0, 0)),
            pl.BlockSpec((1, gl), lambda i, n: (0, 0)),
            pl.BlockSpec((CONV_K, tn), lambda i, n: (0, jnp.minimum(n, n_conv - 1))),
        ],
        out_specs=[
            pl.BlockSpec((tm, tn), lambda i, n: (i, n)),
            pl.BlockSpec((tm, gl), lambda i, n: (i, 0)),
        ],
        out_shape=[
            jax.ShapeDtypeStruct((t, n_total), BF16),
            jax.ShapeDtypeStruct((t, gl), F32),
        ],
        scratch_shapes=[
            pltpu.VMEM((tm, d), BF16),
            pltpu.VMEM((CONV_HALO + tm, tn), F32),
            pltpu.VMEM((n_conv, CONV_HALO, tn), F32),
        ],
        compiler_params=_params(("arbitrary", "arbitrary")),
        name="gdnin",
    )(x, mod, nw, w_main, w_gate, alog_l, dtb_l, conv_w)


def _gdn_kernel(q_ref, k_ref, v_ref, z_ref, gt_ref, nw_ref, o_ref, state_ref, *, n_chunks):
    @pl.when(pl.program_id(2) == 0)
    def _():
        state_ref[...] = jnp.zeros_like(state_ref)

    c = CHUNK
    hd = HEAD_DIM
    row = lax.broadcasted_iota(jnp.int32, (c, 2 * c), 0)
    lane = lax.broadcasted_iota(jnp.int32, (c, 2 * c), 1)
    col = lane % c
    left = lane < c
    tril = row >= col
    strict = row > col
    eye2 = (row == col).astype(F32)
    br = lax.broadcasted_iota(jnp.int32, (2 * c, 2 * c), 0) // c
    bc = lax.broadcasted_iota(jnp.int32, (2 * c, 2 * c), 1) // c
    same_block = br == bc

    def blockdiag(p):
        return jnp.where(same_block, jnp.concatenate([p, p], axis=0), 0.0).astype(BF16)

    nw = nw_ref[...]
    zeros_b = jnp.zeros((c, hd), BF16)
    n_pairs = GDN_GROUP // 2
    items = [(cc, jq) for cc in range(n_chunks) for jq in range(n_pairs)]

    ts = n_chunks * c
    raw = gt_ref[0]
    rr = lax.broadcasted_iota(jnp.int32, (ts, ts), 0)
    rc = lax.broadcasted_iota(jnp.int32, (ts, ts), 1)
    tri = jnp.where((rr >= rc) & ((rr // c) == (rc // c)), 1.0, 0.0).astype(BF16)
    gates_all = jnp.where(lax.broadcasted_iota(jnp.int32, raw.shape, 1) < GDN_GROUP, _tri_cumsum(tri, raw), raw)

    chunk = []
    for cc in range(n_chunks):
        rows = slice(cc * c, (cc + 1) * c)
        gates = gates_all[rows, :]
        g_last = gates[c - 1:c, :]
        chunk.append(dict(
            rows=rows, gates=gates,
            gates_t=jnp.concatenate([gates, gates], axis=0).T,
            exp_g=jnp.exp(gates),
            exp_rest=jnp.exp(g_last - gates),
            exp_last=jnp.exp(g_last)))

    it = {}
    for key in items:
        cc, jq = key
        ck = chunk[cc]
        gates, gates_t, rows = ck["gates"], ck["gates_t"], ck["rows"]
        ha, hb = 2 * jq, 2 * jq + 1
        q = q_ref[0, rows, jq * hd:(jq + 1) * hd]
        k = k_ref[0, rows, jq * hd:(jq + 1) * hd]
        kq = lax.dot_general(jnp.concatenate([k, q], axis=0), jnp.concatenate([k, k], axis=0),
                             (((1,), (1,)), ((), ())), preferred_element_type=F32)
        g_col = jnp.where(left, gates[:, ha:ha + 1], gates[:, hb:hb + 1])
        g_row = jnp.where(left[0:1], gates_t[ha:ha + 1, :], gates_t[hb:hb + 1, :])
        beta_a = gates[:, GDN_GROUP + ha:GDN_GROUP + ha + 1]
        beta_b = gates[:, GDN_GROUP + hb:GDN_GROUP + hb + 1]
        decay = jnp.where(tril, jnp.exp(g_col - g_row), 0.0)
        a_mat = jnp.where(strict, jnp.where(left, beta_a, beta_b) * kq[:c] * decay, 0.0)
        it[key] = dict(q=q, k=k, attn=(kq[c:] * decay).astype(BF16), p=-a_mat, x=eye2 - a_mat,
                       heads=((ha, beta_a), (hb, beta_b)))

    for _ in range(5):
        for key in items:
            d = it[key]
            d["p"] = _bdot(d["p"].astype(BF16), blockdiag(d["p"]))
        for key in items:
            d = it[key]
            d["x"] = d["x"] + _bdot(d["x"].astype(BF16), blockdiag(d["p"]))

    for key in items:
        d = it[key]
        ck = chunk[key[0]]
        kf = d["k"].astype(F32)
        vbs, kbs = [], []
        for h, beta in d["heads"]:
            vf = v_ref[0, ck["rows"], h * hd:(h + 1) * hd].astype(F32)
            vbs.append((vf * beta).astype(BF16))
            kbs.append((kf * (beta * ck["exp_g"][:, h:h + 1])).astype(BF16))
        rhs = jnp.concatenate([
            jnp.concatenate([vbs[0], zeros_b, kbs[0], zeros_b], axis=1),
            jnp.concatenate([zeros_b, vbs[1], zeros_b, kbs[1]], axis=1)], axis=0)
        d["uw"] = _bdot(d["x"].astype(BF16), rhs)
        d["kg"] = jnp.concatenate([(kf * ck["exp_rest"][:, h:h + 1]).astype(BF16) for h, _ in d["heads"]], axis=0)
        qf = d["q"].astype(F32)
        d["qg"] = [(qf * ck["exp_g"][:, h:h + 1]).astype(BF16) for h, _ in d["heads"]]

    states = [state_ref[h] for h in range(GDN_GROUP)]
    for cc in range(n_chunks):
        ck = chunk[cc]
        wqs = {}
        for jq in range(n_pairs):
            d = it[(cc, jq)]
            for r, (h, _) in enumerate(d["heads"]):
                w_r = d["uw"][:, (2 + r) * hd:(3 + r) * hd].astype(BF16)
                wqs[h] = _bdot(jnp.concatenate([w_r, d["qg"][r]], axis=0), states[h].astype(BF16))
        outs = {}
        for jq in range(n_pairs):
            d = it[(cc, jq)]
            (ha, _), (hb, _) = d["heads"]
            vna = (d["uw"][:, 0:hd] - wqs[ha][:c]).astype(BF16)
            vnb = (d["uw"][:, hd:2 * hd] - wqs[hb][:c]).astype(BF16)
            vn = jnp.concatenate([jnp.concatenate([vna, zeros_b], axis=1),
                                  jnp.concatenate([zeros_b, vnb], axis=1)], axis=0)
            o_intra = _bdot(d["attn"], vn)
            ds = lax.dot_general(d["kg"], vn, (((0,), (0,)), ((), ())), preferred_element_type=F32)
            for r, h in enumerate((ha, hb)):
                states[h] = states[h] * ck["exp_last"][0:1, h:h + 1] + ds[:, r * hd:(r + 1) * hd]
                outs[h] = wqs[h][c:] + o_intra[:, r * hd:(r + 1) * hd]
        for h in range(GDN_GROUP):
            o = outs[h]
            on = o * lax.rsqrt(jnp.mean(o * o, axis=-1, keepdims=True) + EPS) * nw
            zf = z_ref[0, ck["rows"], h * hd:(h + 1) * hd].astype(F32)
            o_ref[0, ck["rows"], h * hd:(h + 1) * hd] = (on * _silu(zf)).astype(BF16)

    for h in range(GDN_GROUP):
        state_ref[h] = states[h]


def _gdn(qkvz, gates, norm_w, *, batch, seq, qk_dim, v_dim, n_chunks=4):
    total = qkvz.shape[1]
    x3 = qkvz.reshape(batch, seq, total)
    g3 = gates.reshape(batch, seq, gates.shape[1])
    hd = HEAD_DIM
    qw = GDN_GROUP // 2 * hd
    vw = GDN_GROUP * hd
    n_groups = v_dim // vw
    k_off = qk_dim // qw
    v_off = 2 * qk_dim // vw
    z_off = (2 * qk_dim + v_dim) // vw
    ts = n_chunks * CHUNK
    return pl.pallas_call(
        functools.partial(_gdn_kernel, n_chunks=n_chunks),
        grid=(batch, n_groups, seq // ts),
        in_specs=[
            pl.BlockSpec((1, ts, qw), lambda b, g, c: (b, c, g)),
            pl.BlockSpec((1, ts, qw), lambda b, g, c: (b, c, k_off + g)),
            pl.BlockSpec((1, ts, vw), lambda b, g, c: (b, c, v_off + g)),
            pl.BlockSpec((1, ts, vw), lambda b, g, c: (b, c, z_off + g)),
            pl.BlockSpec((1, ts, LANES), lambda b, g, c: (b, c, g)),
            pl.BlockSpec((1, hd), lambda b, g, c: (0, 0)),
        ],
        out_specs=pl.BlockSpec((1, ts, vw), lambda b, g, c: (b, c, g)),
        out_shape=jax.ShapeDtypeStruct((batch, seq, v_dim), BF16),
        scratch_shapes=[pltpu.VMEM((GDN_GROUP, hd, hd), F32)],
        compiler_params=_params(("parallel", "parallel", "arbitrary")),
        name="gdn",
    )(x3, x3, x3, x3, g3, norm_w)


def _grouped_gate_layout(w_b, w_a, a_log, dt_bias):
    d, hv = w_a.shape
    ng = hv // GDN_GROUP
    pad = LANES - 2 * GDN_GROUP
    wa = w_a.reshape(d, ng, GDN_GROUP)
    wb = w_b.reshape(d, ng, GDN_GROUP)
    w = jnp.concatenate([wa, wb, jnp.zeros((d, ng, pad), w_a.dtype)], axis=-1).reshape(d, ng * LANES)
    zg = jnp.zeros((ng, LANES - GDN_GROUP), F32)
    alog = jnp.concatenate([a_log.reshape(ng, GDN_GROUP).astype(F32), zg], axis=-1).reshape(1, ng * LANES)
    dtb = jnp.concatenate([dt_bias.reshape(ng, GDN_GROUP).astype(F32), zg], axis=-1).reshape(1, ng * LANES)
    return w.astype(BF16), alog, dtb


def kernel(x, c, ada_w, ada_b, norm_w, ffn_w_gu, ffn_w_down, ab_w_in, ab_b_f, pool_w, pool_scale, ab_w_out,
           gdn_w_in, gdn_conv_w, gdn_a_log, gdn_dt_bias, gdn_norm_w, gdn_w_out, final_norm_w):
    batch, seq, d = x.shape
    depth = ada_w.shape[0]
    t = batch * seq

    mod_all = _adaln(c, ada_w, ada_b).reshape(depth, batch, N_MOD, d)
    w_gu = ffn_w_gu.astype(BF16)
    w_down = ffn_w_down.astype(BF16)
    fw = final_norm_w.reshape(1, d)
    xf = x.reshape(t, d)

    for layer in range(depth):
        mod = mod_all[layer]
        e = layer // 2
        nws = norm_w[layer].reshape(3, 1, d)
        xf = _ffn(xf, mod, nws[0], w_gu[layer, 0], w_down[layer, 0], fw, k0=0, seq=seq, final=False)

        if layer % 2 == 0:
            pool_width = pool_scale.shape[1]
            fox_heads = ab_b_f.shape[1]
            fox_width = fox_heads * HEAD_DIM
            n_main = pool_width + 3 * fox_width
            w_in = ab_w_in[e]
            w_main = w_in[:, :n_main].astype(BF16)
            w_f = jnp.pad(w_in[:, n_main:], ((0, 0), (0, LANES - fox_heads))).astype(BF16)
            b_f = jnp.pad(ab_b_f[e], (0, LANES - fox_heads)).reshape(1, LANES)
            pool_out, qkv, cum = _abin(xf, mod, nws[1], w_main, w_f, b_f, pool_w[e].astype(BF16),
                                       pool_scale[e].reshape(1, pool_width), k0=3, seq=seq)
            tq = 512
            cum_t = cum.reshape(batch, seq, LANES)[:, :, :fox_heads].transpose(0, 2, 1)
            attn = _fox(qkv, cum_t.reshape(batch, fox_heads, seq, 1),
                        cum_t.reshape(batch * fox_heads, seq // tq, 1, tq),
                        batch=batch, seq=seq, heads=fox_heads, tq=tq)
            w_out = ab_w_out[e].astype(BF16)
            xf = _proj_res(xf, mod, [pool_out, attn.reshape(t, fox_width)],
                           [w_out[:pool_width], w_out[pool_width:]], gate_row=5, seq=seq)
        else:
            v_heads = gdn_a_log.shape[1]
            v_dim = v_heads * HEAD_DIM
            conv_ch = gdn_conv_w.shape[2]
            qk_dim = (conv_ch - v_dim) // 2
            n_main = conv_ch + v_dim
            w_in = gdn_w_in[e]
            w_gate, alog_l, dtb_l = _grouped_gate_layout(
                w_in[:, n_main:n_main + v_heads], w_in[:, n_main + v_heads:], gdn_a_log[e], gdn_dt_bias[e])
            qkvz, gates = _gdnin(xf, mod, nws[1], w_in[:, :n_main].astype(BF16), w_gate, alog_l, dtb_l,
                                 gdn_conv_w[e], k0=3, seq=seq, qk_dim=qk_dim, conv_ch=conv_ch)
            o = _gdn(qkvz, gates, gdn_norm_w[e].reshape(1, HEAD_DIM), batch=batch, seq=seq,
                     qk_dim=qk_dim, v_dim=v_dim)
            xf = _proj_res(xf, mod, [o.reshape(t, v_dim)], [gdn_w_out[e].astype(BF16)], gate_row=5, seq=seq)

        xf = _ffn(xf, mod, nws[2], w_gu[layer, 1], w_down[layer, 1], fw, k0=6, seq=seq,
                  final=(layer == depth - 1))
    return xf.reshape(batch, seq, d)
```

```python
import functools

import jax
import jax.numpy as jnp
from jax import lax
from jax.experimental import pallas as pl
from jax.experimental.pallas import tpu as pltpu

F32 = jnp.float32
BF16 = jnp.bfloat16
EPS = 1e-6

LANES = 128
MXU_COLS = 256
N_MOD = 9
POOL_WINDOWS = (2, 4, 8, 16)
POOL_HALO = 16
HEAD_DIM = 128
CONV_K = 4
CONV_HALO = 8
CHUNK = 64
GDN_GROUP = 8
NEG_BIG = -1e30
LOG2E = 1.4426950408889634
VMEM_LIMIT = 56 * 1024 * 1024
VMEM_LIMIT_BIG = 62 * 1024 * 1024


def _silu(x):
    return x * jax.nn.sigmoid(x)


def _softplus(x):
    return jnp.maximum(x, 0.0) + jnp.log1p(jnp.exp(-jnp.abs(x)))


def _bdot(a, b):
    return jnp.dot(a, b, preferred_element_type=F32)


def _tri_cumsum(tri, x):
    x1 = x.astype(BF16)
    r1 = x - x1.astype(F32)
    x2 = r1.astype(BF16)
    x3 = (r1 - x2.astype(F32)).astype(BF16)
    s = _bdot(tri, jnp.concatenate([x1, x2, x3], axis=1))
    return s[:, 0:LANES] + s[:, LANES:2 * LANES] + s[:, 2 * LANES:3 * LANES]


def _norm_mod(x, nw, shift, scale):
    y = x * lax.rsqrt(jnp.mean(x * x, axis=-1, keepdims=True) + EPS) * nw
    return y * (1.0 + scale) + shift


def _params(sem, vmem=VMEM_LIMIT):
    return pltpu.CompilerParams(dimension_semantics=sem, vmem_limit_bytes=vmem)


def _adaln_kernel(c_ref, w_ref, b_ref, o_ref):
    ca = _silu(c_ref[...]).astype(BF16)
    o_ref[0] = _bdot(ca, w_ref[0].astype(BF16)) + b_ref[0]


def _adaln(c, ada_w, ada_b):
    depth, d, n = ada_w.shape
    b = c.shape[0]
    tn = 1024
    return pl.pallas_call(
        _adaln_kernel,
        grid=(depth, n // tn),
        in_specs=[
            pl.BlockSpec((b, d), lambda l, j: (0, 0)),
            pl.BlockSpec((1, d, tn), lambda l, j: (l, 0, j)),
            pl.BlockSpec((1, 1, tn), lambda l, j: (l, 0, j)),
        ],
        out_specs=pl.BlockSpec((1, b, tn), lambda l, j: (l, 0, j)),
        out_shape=jax.ShapeDtypeStruct((depth, b, n), F32),
        compiler_params=_params(("parallel", "parallel")),
        name="adaln",
    )(c, ada_w, ada_b.reshape(depth, 1, n))


def _ffn_kernel(x_ref, mod_ref, nw_ref, wg_ref, wu_ref, wd_ref, fw_ref, o_ref, h_ref, *, k0, n_ff, final):
    j = pl.program_id(1)

    @pl.when(j == 0)
    def _():
        h = _norm_mod(x_ref[...], nw_ref[...], mod_ref[0, k0:k0 + 1, :], mod_ref[0, k0 + 1:k0 + 2, :])
        h_ref[...] = h.astype(BF16)
        o_ref[...] = x_ref[...]

    h = h_ref[...]
    g = _bdot(h, wg_ref[...])
    u = _bdot(h, wu_ref[...])
    half_gate = 0.5 * mod_ref[0, k0 + 2:k0 + 3, :]
    o_ref[...] += half_gate * _bdot((_silu(g) * u).astype(BF16), wd_ref[...])

    if final:
        @pl.when(j == n_ff - 1)
        def _():
            xn = o_ref[...]
            o_ref[...] = xn * lax.rsqrt(jnp.mean(xn * xn, axis=-1, keepdims=True) + EPS) * fw_ref[...]


def _ffn(x, mod, nw, w_gu, w_down, fw, *, layer, half, seq, final, tm=1024, tf=512):
    t, d = x.shape
    dff = w_down.shape[2]
    n_ff = dff // tf
    tpb = seq // tm
    k0 = 6 * half
    kern = functools.partial(_ffn_kernel, k0=k0, n_ff=n_ff, final=final)
    return pl.pallas_call(
        kern,
        grid=(t // tm, n_ff),
        in_specs=[
            pl.BlockSpec((tm, d), lambda i, j: (i, 0)),
            pl.BlockSpec((1, N_MOD, d), lambda i, j: (i // tpb, 0, 0)),
            pl.BlockSpec((None, None, 1, d), lambda i, j: (layer, 2 * half, 0, 0)),
            pl.BlockSpec((None, None, d, tf), lambda i, j: (layer, half, 0, j)),
            pl.BlockSpec((None, None, d, tf), lambda i, j: (layer, half, 0, j + n_ff)),
            pl.BlockSpec((None, None, tf, d), lambda i, j: (layer, half, j, 0)),
            pl.BlockSpec((1, d), lambda i, j: (0, 0)),
        ],
        out_specs=pl.BlockSpec((tm, d), lambda i, j: (i, 0)),
        out_shape=jax.ShapeDtypeStruct((t, d), F32),
        scratch_shapes=[pltpu.VMEM((tm, d), BF16)],
        compiler_params=_params(("parallel", "arbitrary"), vmem=VMEM_LIMIT_BIG),
        name="ffn",
    )(x, mod, nw, w_gu, w_gu, w_down, fw)


def _abin_kernel(x_ref, mod_ref, nw_ref, w_ref, wf_ref, bf_ref, pw_ref, ps_ref,
                 pool_ref, qkv_ref, cum_ref, h_ref, ubuf, fcarry, *, k0, tm, tpb, gw):
    i = pl.program_id(0)
    n = pl.program_id(1)
    first = (i % tpb) == 0

    @pl.when(n == 0)
    def _():
        h = _norm_mod(x_ref[...], nw_ref[...], mod_ref[0, k0:k0 + 1, :], mod_ref[0, k0 + 1:k0 + 2, :])
        h_ref[...] = h.astype(BF16)
        hb = h_ref[...]

        f = _bdot(hb, wf_ref[...]) + bf_ref[...]
        log_f = jnp.minimum(f, 0.0) - jnp.log1p(jnp.exp(-jnp.abs(f)))
        r = lax.broadcasted_iota(jnp.int32, (tm, tm), 0)
        c = lax.broadcasted_iota(jnp.int32, (tm, tm), 1)
        tri = jnp.where(r >= c, 1.0, 0.0).astype(BF16)

        @pl.when(first)
        def _():
            fcarry[...] = jnp.zeros_like(fcarry)
            ubuf[0:POOL_HALO, :] = jnp.zeros((POOL_HALO, ubuf.shape[1]), F32)

        cum = _tri_cumsum(tri, log_f) + fcarry[...]
        cum_ref[...] = cum
        fcarry[...] = cum[tm - 1:tm, :]

        u = _bdot(hb, w_ref[...])
        ubuf[POOL_HALO:POOL_HALO + tm, :] = u
        pos = (i % tpb) * tm + lax.broadcasted_iota(jnp.int32, (tm, 1), 0)
        for g, win in enumerate(POOL_WINDOWS):
            sl = slice(g * gw, (g + 1) * gw)
            acc = u[:, sl]
            for s in range(1, win):
                acc = acc + ubuf[POOL_HALO - s:POOL_HALO - s + tm, sl]
            cnt = jnp.minimum(pos + 1, win).astype(F32)
            dlt = acc / cnt - u[:, sl]
            y = _bdot(dlt.astype(BF16), pw_ref[g]) * ps_ref[:, sl]
            pool_ref[:, sl] = y.astype(BF16)
        ubuf[0:POOL_HALO, :] = ubuf[tm:tm + POOL_HALO, :]

    @pl.when(n > 0)
    def _():
        col_scale = jnp.where(n == 1, LOG2E * HEAD_DIM ** -0.5, 1.0).astype(F32)
        qkv_ref[...] = (_bdot(h_ref[...], w_ref[...]) * col_scale).astype(BF16)


def _abin(x, mod, nw, w_main, w_f, b_f, pool_w, pool_scale, *, k0, seq, tm=512):
    t, d = x.shape
    tn = pool_scale.shape[1]
    tpb = seq // tm
    groups = pool_w.shape[0]
    gw = tn // groups
    kern = functools.partial(_abin_kernel, k0=k0, tm=tm, tpb=tpb, gw=gw)
    return pl.pallas_call(
        kern,
        grid=(t // tm, 4),
        in_specs=[
            pl.BlockSpec((tm, d), lambda i, n: (i, 0)),
            pl.BlockSpec((1, N_MOD, d), lambda i, n: (i // tpb, 0, 0)),
            pl.BlockSpec((1, d), lambda i, n: (0, 0)),
            pl.BlockSpec((d, tn), lambda i, n: (0, n)),
            pl.BlockSpec((d, LANES), lambda i, n: (0, 0)),
            pl.BlockSpec((1, LANES), lambda i, n: (0, 0)),
            pl.BlockSpec((groups, gw, gw), lambda i, n: (0, 0, 0)),
            pl.BlockSpec((1, tn), lambda i, n: (0, 0)),
        ],
        out_specs=[
            pl.BlockSpec((tm, tn), lambda i, n: (i, 0)),
            pl.BlockSpec((tm, tn), lambda i, n: (i, jnp.maximum(n - 1, 0))),
            pl.BlockSpec((tm, LANES), lambda i, n: (i, 0)),
        ],
        out_shape=[
            jax.ShapeDtypeStruct((t, tn), BF16),
            jax.ShapeDtypeStruct((t, 3 * tn), BF16),
            jax.ShapeDtypeStruct((t, LANES), F32),
        ],
        scratch_shapes=[
            pltpu.VMEM((tm, d), BF16),
            pltpu.VMEM((POOL_HALO + tm, tn), F32),
            pltpu.VMEM((1, LANES), F32),
        ],
        compiler_params=_params(("arbitrary", "arbitrary")),
        name="abin",
    )(x, mod, nw, w_main, w_f, b_f, pool_w, pool_scale)


def _fox_kernel(q_ref, k_ref, v_ref, cum_ref, o_ref, qa_ref, ka_ref, va_ref, *, tq, seq):
    h = pl.program_id(1)
    hd = HEAD_DIM
    cum = cum_ref[0]
    lane = lax.broadcasted_iota(jnp.int32, cum.shape, 1)
    f = jnp.sum(jnp.where(lane == h, cum, 0.0), axis=-1, keepdims=True) * LOG2E
    f_hi = f.astype(BF16).astype(F32)
    r1 = f - f_hi
    f_mid = r1.astype(BF16).astype(F32)
    f_lo = r1 - f_mid
    zero = jnp.zeros_like(cum)
    one_or_zero = jnp.where(lane < 6, 1.0, 0.0)
    fq = jnp.where(lane == 0, f_hi, jnp.where(lane == 1, f_mid, jnp.where(lane == 2, f_lo, one_or_zero)))
    fk = jnp.where(lane < 3, 1.0,
                   jnp.where(lane == 3, -f_hi, jnp.where(lane == 4, -f_mid, jnp.where(lane == 5, -f_lo, zero))))
    qa_ref[:, 0:hd] = q_ref[0]
    qa_ref[:, hd:2 * hd] = fq.astype(BF16)
    ka_ref[:, 0:hd] = k_ref[0]
    ka_ref[:, hd:2 * hd] = fk.astype(BF16)
    va_ref[:, 0:hd] = v_ref[0]
    va_ref[:, hd:2 * hd] = jnp.ones((seq, hd), BF16)

    r = lax.broadcasted_iota(jnp.int32, (tq, tq), 0)
    c = lax.broadcasted_iota(jnp.int32, (tq, tq), 1)
    causal = c <= r

    nq = seq // tq
    m = [jnp.full((tq, 1), NEG_BIG, F32) for _ in range(nq)]
    acc = [jnp.zeros((tq, 2 * hd), F32) for _ in range(nq)]
    for j in range(nq):
        ka = ka_ref[j * tq:(j + 1) * tq, :]
        va = va_ref[j * tq:(j + 1) * tq, :]
        for qi in range(j, nq):
            s = lax.dot_general(qa_ref[qi * tq:(qi + 1) * tq, :], ka, (((1,), (1,)), ((), ())),
                                preferred_element_type=F32)
            if qi == j:
                s = jnp.where(causal, s, NEG_BIG)
            m_new = jnp.maximum(m[qi], jnp.max(s, axis=-1, keepdims=True))
            p = jnp.exp2(s - m_new)
            acc[qi] = jnp.exp2(m[qi] - m_new) * acc[qi] + _bdot(p.astype(BF16), va)
            m[qi] = m_new
    for qi in range(nq):
        o_ref[0, qi * tq:(qi + 1) * tq, :] = (acc[qi][:, 0:hd] / acc[qi][:, hd:2 * hd]).astype(BF16)


def _fox(qkv, cum, *, batch, seq, heads, tq=256):
    qkv3 = qkv.reshape(batch, seq, 3 * heads * HEAD_DIM)
    cum3 = cum.reshape(batch, seq, LANES)
    kern = functools.partial(_fox_kernel, tq=tq, seq=seq)
    return pl.pallas_call(
        kern,
        grid=(batch, heads),
        in_specs=[
            pl.BlockSpec((1, seq, HEAD_DIM), lambda b, h: (b, 0, h)),
            pl.BlockSpec((1, seq, HEAD_DIM), lambda b, h: (b, 0, heads + h)),
            pl.BlockSpec((1, seq, HEAD_DIM), lambda b, h: (b, 0, 2 * heads + h)),
            pl.BlockSpec((1, seq, LANES), lambda b, h: (b, 0, 0)),
        ],
        out_specs=pl.BlockSpec((1, seq, HEAD_DIM), lambda b, h: (b, 0, h)),
        out_shape=jax.ShapeDtypeStruct((batch, seq, heads * HEAD_DIM), BF16),
        scratch_shapes=[pltpu.VMEM((seq, 2 * HEAD_DIM), BF16)] * 3,
        compiler_params=_params(("parallel", "parallel")),
        name="fox",
    )(qkv3, qkv3, qkv3, cum3)


def _proj_res_kernel(*refs, n_in, gate_row):
    x_ref, mod_ref = refs[0], refs[1]
    a_refs = refs[2:2 + n_in]
    w_ref, o_ref = refs[2 + n_in], refs[3 + n_in]
    y = None
    row = 0
    for a_ref in a_refs:
        ka = a_ref.shape[1]
        part = _bdot(a_ref[...], w_ref[row:row + ka, :])
        y = part if y is None else y + part
        row += ka
    o_ref[...] = x_ref[...] + mod_ref[0, gate_row:gate_row + 1, :] * y


def _proj_res(x, mod, acts, weight, *, gate_row, seq, tm=256):
    t, d = x.shape
    tpb = seq // tm
    n_in = len(acts)
    kern = functools.partial(_proj_res_kernel, n_in=n_in, gate_row=gate_row)
    in_specs = [
        pl.BlockSpec((tm, d), lambda i: (i, 0)),
        pl.BlockSpec((1, N_MOD, d), lambda i: (i // tpb, 0, 0)),
    ]
    in_specs += [pl.BlockSpec((tm, a.shape[1]), lambda i: (i, 0)) for a in acts]
    in_specs += [pl.BlockSpec(weight.shape, lambda i: (0, 0), pipeline_mode=pl.Buffered(1))]
    return pl.pallas_call(
        kern,
        grid=(t // tm,),
        in_specs=in_specs,
        out_specs=pl.BlockSpec((tm, d), lambda i: (i, 0)),
        out_shape=jax.ShapeDtypeStruct((t, d), F32),
        compiler_params=_params(("parallel",)),
        name="proj_res",
    )(x, mod, *acts, weight)


def _gdnin_kernel(x_ref, mod_ref, nw_ref, w_ref, wg_ref, alog_ref, dtb_ref, cw_ref,
                  o_ref, gate_ref, h_ref, pbuf, carry_ref, *, k0, tm, tpb, n_conv, n_qk, n_q):
    i = pl.program_id(0)
    n = pl.program_id(1)
    first = (i % tpb) == 0

    @pl.when(n == 0)
    def _():
        h = _norm_mod(x_ref[...], nw_ref[...], mod_ref[0, k0:k0 + 1, :], mod_ref[0, k0 + 1:k0 + 2, :])
        h_ref[...] = h.astype(BF16)
        ba = _bdot(h_ref[...], wg_ref[...])
        lane = lax.broadcasted_iota(jnp.int32, ba.shape, 1) % LANES
        g = -jnp.exp(alog_ref[...]) * _softplus(ba + dtb_ref[...])
        gate_ref[...] = jnp.where(lane < GDN_GROUP, g, jax.nn.sigmoid(ba))

        @pl.when(i == 0)
        def _():
            carry_ref[...] = jnp.zeros_like(carry_ref)

    tn = w_ref.shape[1]
    n_sub = tn // MXU_COLS

    def matmul(hb, j):
        return _bdot(hb, w_ref[:, j * MXU_COLS:(j + 1) * MXU_COLS])

    def conv_silu(j, acc):
        cols = slice(j * MXU_COLS, (j + 1) * MXU_COLS)
        pbuf[0:CONV_HALO, cols] = jnp.where(first, 0.0, carry_ref[n, :, cols])
        pbuf[CONV_HALO:CONV_HALO + tm, cols] = acc
        carry_ref[n, :, cols] = acc[tm - CONV_HALO:tm, :]
        y = cw_ref[CONV_K - 1:CONV_K, cols] * acc
        for kk in range(CONV_K - 1):
            lo = CONV_HALO - (CONV_K - 1) + kk
            y = y + cw_ref[kk:kk + 1, cols] * pbuf[lo:lo + tm, cols]
        return _silu(y)

    def pipelined(epilogue):
        hb = h_ref[...]
        acc = matmul(hb, 0)
        for j in range(1, n_sub):
            nxt = matmul(hb, j)
            epilogue(j - 1, acc)
            acc = nxt
        epilogue(n_sub - 1, acc)

    @pl.when(n < n_qk)
    def _():
        qscale = jnp.where(n < n_q, HEAD_DIM ** -0.5, 1.0).astype(F32)

        def epilogue(j, acc):
            y = conv_silu(j, acc)
            for hh in range(MXU_COLS // HEAD_DIM):
                seg = y[:, hh * HEAD_DIM:(hh + 1) * HEAD_DIM]
                inv = lax.rsqrt(jnp.sum(seg * seg, axis=-1, keepdims=True) + EPS) * qscale
                lo = j * MXU_COLS + hh * HEAD_DIM
                o_ref[:, lo:lo + HEAD_DIM] = (seg * inv).astype(BF16)

        pipelined(epilogue)

    @pl.when((n >= n_qk) & (n < n_conv))
    def _():
        def epilogue(j, acc):
            o_ref[:, j * MXU_COLS:(j + 1) * MXU_COLS] = conv_silu(j, acc).astype(BF16)

        pipelined(epilogue)

    @pl.when(n >= n_conv)
    def _():
        o_ref[...] = _bdot(h_ref[...], w_ref[...]).astype(BF16)


def _gdnin(x, mod, nw, w_main, w_gate, alog_l, dtb_l, conv_w, *, k0, seq, qk_dim, conv_ch, n_main,
           tm=512, tn=1024):
    t, d = x.shape
    n_total = n_main
    tpb = seq // tm
    n_conv = conv_ch // tn
    n_qk = 2 * qk_dim // tn
    n_q = qk_dim // tn
    gl = w_gate.shape[1]
    kern = functools.partial(_gdnin_kernel, k0=k0, tm=tm, tpb=tpb, n_conv=n_conv, n_qk=n_qk, n_q=n_q)
    return pl.pallas_call(
        kern,
        grid=(t // tm, n_total // tn),
        in_specs=[
            pl.BlockSpec((tm, d), lambda i, n: (i, 0)),
            pl.BlockSpec((1, N_MOD, d), lambda i, n: (i // tpb, 0, 0)),
            pl.BlockSpec((1, d), lambda i, n: (0, 0)),
            pl.BlockSpec((d, tn), lambda i, n: (0, n)),
            pl.BlockSpec((d, gl), lambda i, n: (0, 0)),
            pl.BlockSpec((1, gl), lambda i, n: (0, 0)),
            pl.BlockSpec((1, gl), lambda i, n: (0, 0)),
            pl.BlockSpec((CONV_K, tn), lambda i, n: (0, jnp.minimum(n, n_conv - 1))),
        ],
        out_specs=[
            pl.BlockSpec((tm, tn), lambda i, n: (i, n)),
            pl.BlockSpec((tm, gl), lambda i, n: (i, 0)),
        ],
        out_shape=[
            jax.ShapeDtypeStruct((t, n_total), BF16),
            jax.ShapeDtypeStruct((t, gl), F32),
        ],
        scratch_shapes=[
            pltpu.VMEM((tm, d), BF16),
            pltpu.VMEM((CONV_HALO + tm, tn), F32),
            pltpu.VMEM((n_conv, CONV_HALO, tn), F32),
        ],
        compiler_params=_params(("arbitrary", "arbitrary")),
        name="gdnin",
    )(x, mod, nw, w_main, w_gate, alog_l, dtb_l, conv_w)


def _gdn_kernel(q_ref, k_ref, v_ref, z_ref, gt_ref, nw_ref, o_ref, state_ref, *, n_chunks):
    @pl.when(pl.program_id(2) == 0)
    def _():
        state_ref[...] = jnp.zeros_like(state_ref)

    c = CHUNK
    hd = HEAD_DIM
    row = lax.broadcasted_iota(jnp.int32, (c, 2 * c), 0)
    lane = lax.broadcasted_iota(jnp.int32, (c, 2 * c), 1)
    col = lane % c
    left = lane < c
    tril = row >= col
    strict = row > col
    eye2 = (row == col).astype(F32)
    br = lax.broadcasted_iota(jnp.int32, (2 * c, 2 * c), 0) // c
    bc = lax.broadcasted_iota(jnp.int32, (2 * c, 2 * c), 1) // c
    same_block = br == bc

    def blockdiag(p):
        return jnp.where(same_block, jnp.concatenate([p, p], axis=0), 0.0).astype(BF16)

    nw = nw_ref[...]
    zeros_b = jnp.zeros((c, hd), BF16)
    n_pairs = GDN_GROUP // 2
    items = [(cc, jq) for cc in range(n_chunks) for jq in range(n_pairs)]

    ts = n_chunks * c
    raw = gt_ref[0]
    rr = lax.broadcasted_iota(jnp.int32, (ts, ts), 0)
    rc = lax.broadcasted_iota(jnp.int32, (ts, ts), 1)
    tri = jnp.where((rr >= rc) & ((rr // c) == (rc // c)), 1.0, 0.0).astype(BF16)
    gates_all = jnp.where(lax.broadcasted_iota(jnp.int32, raw.shape, 1) < GDN_GROUP, _tri_cumsum(tri, raw), raw)

    chunk = []
    for cc in range(n_chunks):
        rows = slice(cc * c, (cc + 1) * c)
        gates = gates_all[rows, :]
        g_last = gates[c - 1:c, :]
        chunk.append(dict(
            rows=rows, gates=gates,
            gates_t=jnp.concatenate([gates, gates], axis=0).T,
            exp_g=jnp.exp(gates),
            exp_rest=jnp.exp(g_last - gates),
            exp_last=jnp.exp(g_last)))

    it = {}
    for key in items:
        cc, jq = key
        ck = chunk[cc]
        gates, gates_t, rows = ck["gates"], ck["gates_t"], ck["rows"]
        ha, hb = 2 * jq, 2 * jq + 1
        q = q_ref[0, rows, jq * hd:(jq + 1) * hd]
        k = k_ref[0, rows, jq * hd:(jq + 1) * hd]
        kq = lax.dot_general(jnp.concatenate([k, q], axis=0), jnp.concatenate([k, k], axis=0),
                             (((1,), (1,)), ((), ())), preferred_element_type=F32)
        g_col = jnp.where(left, gates[:, ha:ha + 1], gates[:, hb:hb + 1])
        g_row = jnp.where(left[0:1], gates_t[ha:ha + 1, :], gates_t[hb:hb + 1, :])
        beta_a = gates[:, GDN_GROUP + ha:GDN_GROUP + ha + 1]
        beta_b = gates[:, GDN_GROUP + hb:GDN_GROUP + hb + 1]
        decay = jnp.where(tril, jnp.exp(g_col - g_row), 0.0)
        a_mat = jnp.where(strict, jnp.where(left, beta_a, beta_b) * kq[:c] * decay, 0.0)
        it[key] = dict(q=q, k=k, attn=(kq[c:] * decay).astype(BF16), p=-a_mat, x=eye2 - a_mat,
                       heads=((ha, beta_a), (hb, beta_b)))

    for _ in range(5):
        for key in items:
            d = it[key]
            d["p"] = _bdot(d["p"].astype(BF16), blockdiag(d["p"]))
        for key in items:
            d = it[key]
            d["x"] = d["x"] + _bdot(d["x"].astype(BF16), blockdiag(d["p"]))

    for key in items:
        d = it[key]
        ck = chunk[key[0]]
        kf = d["k"].astype(F32)
        vbs, kbs = [], []
        for h, beta in d["heads"]:
            vf = v_ref[0, ck["rows"], h * hd:(h + 1) * hd].astype(F32)
            vbs.append((vf * beta).astype(BF16))
            kbs.append((kf * (beta * ck["exp_g"][:, h:h + 1])).astype(BF16))
        rhs = jnp.concatenate([
            jnp.concatenate([vbs[0], zeros_b, kbs[0], zeros_b], axis=1),
            jnp.concatenate([zeros_b, vbs[1], zeros_b, kbs[1]], axis=1)], axis=0)
        d["uw"] = _bdot(d["x"].astype(BF16), rhs)
        d["kg"] = jnp.concatenate([(kf * ck["exp_rest"][:, h:h + 1]).astype(BF16) for h, _ in d["heads"]], axis=0)
        qf = d["q"].astype(F32)
        d["qg"] = [(qf * ck["exp_g"][:, h:h + 1]).astype(BF16) for h, _ in d["heads"]]

    states = [state_ref[h] for h in range(GDN_GROUP)]
    for cc in range(n_chunks):
        ck = chunk[cc]
        wqs = {}
        for jq in range(n_pairs):
            d = it[(cc, jq)]
            for r, (h, _) in enumerate(d["heads"]):
                w_r = d["uw"][:, (2 + r) * hd:(3 + r) * hd].astype(BF16)
                wqs[h] = _bdot(jnp.concatenate([w_r, d["qg"][r]], axis=0), states[h].astype(BF16))
        outs = {}
        for jq in range(n_pairs):
            d = it[(cc, jq)]
            (ha, _), (hb, _) = d["heads"]
            vna = (d["uw"][:, 0:hd] - wqs[ha][:c]).astype(BF16)
            vnb = (d["uw"][:, hd:2 * hd] - wqs[hb][:c]).astype(BF16)
            vn = jnp.concatenate([jnp.concatenate([vna, zeros_b], axis=1),
                                  jnp.concatenate([zeros_b, vnb], axis=1)], axis=0)
            o_intra = _bdot(d["attn"], vn)
            ds = lax.dot_general(d["kg"], vn, (((0,), (0,)), ((), ())), preferred_element_type=F32)
            for r, h in enumerate((ha, hb)):
                states[h] = states[h] * ck["exp_last"][0:1, h:h + 1] + ds[:, r * hd:(r + 1) * hd]
                outs[h] = wqs[h][c:] + o_intra[:, r * hd:(r + 1) * hd]
        for h in range(GDN_GROUP):
            o = outs[h]
            on = o * lax.rsqrt(jnp.mean(o * o, axis=-1, keepdims=True) + EPS) * nw
            zf = z_ref[0, ck["rows"], h * hd:(h + 1) * hd].astype(F32)
            o_ref[0, ck["rows"], h * hd:(h + 1) * hd] = (on * _silu(zf)).astype(BF16)

    for h in range(GDN_GROUP):
        state_ref[h] = states[h]


def _gdn(qkvz, gates, norm_w, *, batch, seq, qk_dim, v_dim, n_chunks=8):
    total = qkvz.shape[1]
    x3 = qkvz.reshape(batch, seq, total)
    g3 = gates.reshape(batch, seq, gates.shape[1])
    hd = HEAD_DIM
    qw = GDN_GROUP // 2 * hd
    vw = GDN_GROUP * hd
    n_groups = v_dim // vw
    k_off = qk_dim // qw
    v_off = 2 * qk_dim // vw
    z_off = (2 * qk_dim + v_dim) // vw
    ts = n_chunks * CHUNK
    return pl.pallas_call(
        functools.partial(_gdn_kernel, n_chunks=n_chunks),
        grid=(batch, n_groups, seq // ts),
        in_specs=[
            pl.BlockSpec((1, ts, qw), lambda b, g, c: (b, c, g)),
            pl.BlockSpec((1, ts, qw), lambda b, g, c: (b, c, k_off + g)),
            pl.BlockSpec((1, ts, vw), lambda b, g, c: (b, c, v_off + g)),
            pl.BlockSpec((1, ts, vw), lambda b, g, c: (b, c, z_off + g)),
            pl.BlockSpec((1, ts, LANES), lambda b, g, c: (b, c, g)),
            pl.BlockSpec((1, hd), lambda b, g, c: (0, 0)),
        ],
        out_specs=pl.BlockSpec((1, ts, vw), lambda b, g, c: (b, c, g)),
        out_shape=jax.ShapeDtypeStruct((batch, seq, v_dim), BF16),
        scratch_shapes=[pltpu.VMEM((GDN_GROUP, hd, hd), F32)],
        compiler_params=_params(("parallel", "parallel", "arbitrary")),
        name="gdn",
    )(x3, x3, x3, x3, g3, norm_w)


def _grouped_gate_layout(w_b, w_a, a_log, dt_bias):
    d, hv = w_a.shape
    ng = hv // GDN_GROUP
    pad = LANES - 2 * GDN_GROUP
    wa = w_a.reshape(d, ng, GDN_GROUP)
    wb = w_b.reshape(d, ng, GDN_GROUP)
    w = jnp.concatenate([wa, wb, jnp.zeros((d, ng, pad), w_a.dtype)], axis=-1).reshape(d, ng * LANES)
    zg = jnp.zeros((ng, LANES - GDN_GROUP), F32)
    alog = jnp.concatenate([a_log.reshape(ng, GDN_GROUP).astype(F32), zg], axis=-1).reshape(1, ng * LANES)
    dtb = jnp.concatenate([dt_bias.reshape(ng, GDN_GROUP).astype(F32), zg], axis=-1).reshape(1, ng * LANES)
    return w.astype(BF16), alog, dtb


def kernel(x, c, ada_w, ada_b, norm_w, ffn_w_gu, ffn_w_down, ab_w_in, ab_b_f, pool_w, pool_scale, ab_w_out,
           gdn_w_in, gdn_conv_w, gdn_a_log, gdn_dt_bias, gdn_norm_w, gdn_w_out, final_norm_w):
    batch, seq, d = x.shape
    depth = ada_w.shape[0]
    t = batch * seq

    mod_all = _adaln(c, ada_w, ada_b).reshape(depth, batch, N_MOD, d)
    w_gu = ffn_w_gu.astype(BF16)
    w_down = ffn_w_down.astype(BF16)
    fw = final_norm_w.reshape(1, d)
    xf = x.reshape(t, d)

    nw4 = norm_w.reshape(depth, 3, 1, d)
    for layer in range(depth):
        mod = mod_all[layer]
        e = layer // 2
        nws = nw4[layer]
        xf = _ffn(xf, mod, nw4, w_gu, w_down, fw, layer=layer, half=0, seq=seq, final=False)

        if layer % 2 == 0:
            pool_width = pool_scale.shape[1]
            fox_heads = ab_b_f.shape[1]
            fox_width = fox_heads * HEAD_DIM
            n_main = pool_width + 3 * fox_width
            w_in = ab_w_in[e]
            w_f = jnp.pad(w_in[:, n_main:], ((0, 0), (0, LANES - fox_heads))).astype(BF16)
            b_f = jnp.pad(ab_b_f[e], (0, LANES - fox_heads)).reshape(1, LANES)
            pool_out, qkv, cum = _abin(xf, mod, nws[1], w_in.astype(BF16), w_f, b_f, pool_w[e].astype(BF16),
                                       pool_scale[e].reshape(1, pool_width), k0=3, seq=seq)
            attn = _fox(qkv, cum, batch=batch, seq=seq, heads=fox_heads)
            w_out = ab_w_out[e].astype(BF16)
            xf = _proj_res(xf, mod, [pool_out, attn.reshape(t, fox_width)], w_out, gate_row=5, seq=seq)
        else:
            v_heads = gdn_a_log.shape[1]
            v_dim = v_heads * HEAD_DIM
            conv_ch = gdn_conv_w.shape[2]
            qk_dim = (conv_ch - v_dim) // 2
            n_main = conv_ch + v_dim
            w_in = gdn_w_in[e]
            w_gate, alog_l, dtb_l = _grouped_gate_layout(
                w_in[:, n_main:n_main + v_heads], w_in[:, n_main + v_heads:], gdn_a_log[e], gdn_dt_bias[e])
            qkvz, gates = _gdnin(xf, mod, nws[1], w_in.astype(BF16), w_gate, alog_l, dtb_l,
                                 gdn_conv_w[e], k0=3, seq=seq, qk_dim=qk_dim, conv_ch=conv_ch, n_main=n_main)
            o = _gdn(qkvz, gates, gdn_norm_w[e].reshape(1, HEAD_DIM), batch=batch, seq=seq,
                     qk_dim=qk_dim, v_dim=v_dim)
            xf = _proj_res(xf, mod, [o.reshape(t, v_dim)], gdn_w_out[e].astype(BF16), gate_row=5, seq=seq)

        xf = _ffn(xf, mod, nw4, w_gu, w_down, fw, layer=layer, half=1, seq=seq, final=(layer == depth - 1))
    return xf.reshape(batch, seq, d)
```

```python
import functools

import jax
import jax.numpy as jnp
from jax import lax
from jax.experimental import pallas as pl
from jax.experimental.pallas import tpu as pltpu

F32 = jnp.float32
BF16 = jnp.bfloat16
EPS = 1e-6

LANES = 128
N_MOD = 9
POOL_WINDOWS = (2, 4, 8, 16)
POOL_HALO = 16
HEAD_DIM = 128
CONV_K = 4
CONV_HALO = 8
CHUNK = 64
GDN_GROUP = 8
NEG_BIG = -1e30
LOG2E = 1.4426950408889634
VMEM_LIMIT = 56 * 1024 * 1024
VMEM_LIMIT_BIG = 61 * 1024 * 1024 + 256 * 1024


def _silu(x):
    return x * jax.nn.sigmoid(x)


def _softplus(x):
    return jnp.maximum(x, 0.0) + jnp.log1p(jnp.exp(-jnp.abs(x)))


def _bdot(a, b):
    return jnp.dot(a, b, preferred_element_type=F32)


def _tri_cumsum(tri, x):
    x1 = x.astype(BF16)
    r1 = x - x1.astype(F32)
    x2 = r1.astype(BF16)
    x3 = (r1 - x2.astype(F32)).astype(BF16)
    s = _bdot(tri, jnp.concatenate([x1, x2, x3], axis=1))
    return s[:, 0:LANES] + s[:, LANES:2 * LANES] + s[:, 2 * LANES:3 * LANES]


def _norm_mod(x, nw, shift, scale):
    y = x * lax.rsqrt(jnp.mean(x * x, axis=-1, keepdims=True) + EPS) * nw
    return y * (1.0 + scale) + shift


def _params(sem, vmem=VMEM_LIMIT):
    return pltpu.CompilerParams(dimension_semantics=sem, vmem_limit_bytes=vmem)


def _adaln_kernel(c_ref, w_ref, b_ref, o_ref):
    ca = _silu(c_ref[...]).astype(BF16)
    o_ref[0] = _bdot(ca, w_ref[0].astype(BF16)) + b_ref[0]


def _adaln(c, ada_w, ada_b):
    depth, d, n = ada_w.shape
    b = c.shape[0]
    tn = 1024
    return pl.pallas_call(
        _adaln_kernel,
        grid=(depth, n // tn),
        in_specs=[
            pl.BlockSpec((b, d), lambda l, j: (0, 0)),
            pl.BlockSpec((1, d, tn), lambda l, j: (l, 0, j)),
            pl.BlockSpec((1, 1, tn), lambda l, j: (l, 0, j)),
        ],
        out_specs=pl.BlockSpec((1, b, tn), lambda l, j: (l, 0, j)),
        out_shape=jax.ShapeDtypeStruct((depth, b, n), F32),
        compiler_params=_params(("parallel", "parallel")),
        name="adaln",
    )(c, ada_w, ada_b.reshape(depth, 1, n))


def _ffn_kernel(*refs, k0, n_ff, final, cast_next):
    x_ref, mod_ref, nw_ref, wg_ref, wu_ref, wd_ref, fw_ref = refs[:7]
    if cast_next:
        gu_src, down_src, o_ref, gu_dst, down_dst, h_ref = refs[7:]
        gu_dst[...] = gu_src[...].astype(BF16)
        down_dst[...] = down_src[...].astype(BF16)
    else:
        o_ref, h_ref = refs[7:]
    j = pl.program_id(1)

    @pl.when(j == 0)
    def _():
        h = _norm_mod(x_ref[...], nw_ref[...], mod_ref[0, k0:k0 + 1, :], mod_ref[0, k0 + 1:k0 + 2, :])
        h_ref[...] = h.astype(BF16)
        o_ref[...] = x_ref[...]

    half_gate = 0.5 * mod_ref[0, k0 + 2:k0 + 3, :]
    rows = h_ref.shape[0] // 2
    for r in range(2):
        sl = slice(r * rows, (r + 1) * rows)
        h = h_ref[sl, :]
        g = _bdot(h, wg_ref[...])
        u = _bdot(h, wu_ref[...])
        o_ref[sl, :] += half_gate * _bdot((_silu(g) * u).astype(BF16), wd_ref[...])

    if final:
        @pl.when(j == n_ff - 1)
        def _():
            xn = o_ref[...]
            o_ref[...] = xn * lax.rsqrt(jnp.mean(xn * xn, axis=-1, keepdims=True) + EPS) * fw_ref[...]


def _ffn(x, mod, nw, w_gu, w_down, fw, *, layer, half, seq, final, next_f32=None, tm=1024, tf=512):
    t, d = x.shape
    dff = w_down.shape[0]
    n_i, n_ff = t // tm, dff // tf
    tpb = seq // tm
    k0 = 6 * half
    cast_next = next_f32 is not None
    kern = functools.partial(_ffn_kernel, k0=k0, n_ff=n_ff, final=final, cast_next=cast_next)
    in_specs = [
        pl.BlockSpec((tm, d), lambda i, j: (i, 0)),
        pl.BlockSpec((1, N_MOD, d), lambda i, j: (i // tpb, 0, 0)),
        pl.BlockSpec((None, None, 1, d), lambda i, j: (layer, 2 * half, 0, 0)),
        pl.BlockSpec((d, tf), lambda i, j: (0, j)),
        pl.BlockSpec((d, tf), lambda i, j: (0, j + n_ff)),
        pl.BlockSpec((tf, d), lambda i, j: (j, 0)),
        pl.BlockSpec((1, d), lambda i, j: (0, 0)),
    ]
    out_specs = [pl.BlockSpec((tm, d), lambda i, j: (i, 0))]
    out_shape = [jax.ShapeDtypeStruct((t, d), F32)]
    args = [x, mod, nw, w_gu, w_gu, w_down, fw]
    if cast_next:
        gu32, down32, nl, nh = next_f32
        gu_blk = (d // n_i, 2 * dff // n_ff)
        down_blk = (dff // n_ff, d // n_i)
        in_specs += [pl.BlockSpec((None, None) + gu_blk, lambda i, j: (nl, nh, i, j)),
                     pl.BlockSpec((None, None) + down_blk, lambda i, j: (nl, nh, j, i))]
        out_specs += [pl.BlockSpec(gu_blk, lambda i, j: (i, j)), pl.BlockSpec(down_blk, lambda i, j: (j, i))]
        out_shape += [jax.ShapeDtypeStruct((d, 2 * dff), BF16), jax.ShapeDtypeStruct((dff, d), BF16)]
        args += [gu32, down32]
    outs = pl.pallas_call(
        kern,
        grid=(n_i, n_ff),
        in_specs=in_specs,
        out_specs=out_specs,
        out_shape=out_shape,
        scratch_shapes=[pltpu.VMEM((tm, d), BF16)],
        compiler_params=_params(("parallel", "arbitrary"), vmem=VMEM_LIMIT_BIG),
        name="ffn",
    )(*args)
    return outs if cast_next else outs[0]


def _abin_kernel(x_ref, mod_ref, nw_ref, w_ref, wf_ref, bf_ref, pw_ref, ps_ref,
                 pool_ref, qkv_ref, cum_ref, h_ref, ubuf, fcarry, *, k0, tm, tpb, gw):
    i = pl.program_id(0)
    n = pl.program_id(1)
    first = (i % tpb) == 0

    @pl.when(n == 0)
    def _():
        h = _norm_mod(x_ref[...], nw_ref[...], mod_ref[0, k0:k0 + 1, :], mod_ref[0, k0 + 1:k0 + 2, :])
        h_ref[...] = h.astype(BF16)
        hb = h_ref[...]

        f = _bdot(hb, wf_ref[...]) + bf_ref[...]
        log_f = jnp.minimum(f, 0.0) - jnp.log1p(jnp.exp(-jnp.abs(f)))
        r = lax.broadcasted_iota(jnp.int32, (tm, tm), 0)
        c = lax.broadcasted_iota(jnp.int32, (tm, tm), 1)
        tri = jnp.where(r >= c, 1.0, 0.0).astype(BF16)

        @pl.when(first)
        def _():
            fcarry[...] = jnp.zeros_like(fcarry)
            ubuf[0:POOL_HALO, :] = jnp.zeros((POOL_HALO, ubuf.shape[1]), F32)

        cum = _tri_cumsum(tri, log_f) + fcarry[...]
        cum_ref[...] = cum
        fcarry[...] = cum[tm - 1:tm, :]

        u = _bdot(hb, w_ref[...])
        ubuf[POOL_HALO:POOL_HALO + tm, :] = u
        pos = (i % tpb) * tm + lax.broadcasted_iota(jnp.int32, (tm, 1), 0)
        for g, win in enumerate(POOL_WINDOWS):
            sl = slice(g * gw, (g + 1) * gw)
            acc = u[:, sl]
            for s in range(1, win):
                acc = acc + ubuf[POOL_HALO - s:POOL_HALO - s + tm, sl]
            cnt = jnp.minimum(pos + 1, win).astype(F32)
            dlt = acc / cnt - u[:, sl]
            y = _bdot(dlt.astype(BF16), pw_ref[g]) * ps_ref[:, sl]
            pool_ref[:, sl] = y.astype(BF16)
        ubuf[0:POOL_HALO, :] = ubuf[tm:tm + POOL_HALO, :]

    @pl.when(n > 0)
    def _():
        col_scale = jnp.where(n == 1, LOG2E * HEAD_DIM ** -0.5, 1.0).astype(F32)
        qkv_ref[...] = (_bdot(h_ref[...], w_ref[...]) * col_scale).astype(BF16)


def _abin(x, mod, nw, w_main, w_f, b_f, pool_w, pool_scale, *, k0, seq, tm=512):
    t, d = x.shape
    tn = pool_scale.shape[1]
    tpb = seq // tm
    groups = pool_w.shape[0]
    gw = tn // groups
    kern = functools.partial(_abin_kernel, k0=k0, tm=tm, tpb=tpb, gw=gw)
    return pl.pallas_call(
        kern,
        grid=(t // tm, 4),
        in_specs=[
            pl.BlockSpec((tm, d), lambda i, n: (i, 0)),
            pl.BlockSpec((1, N_MOD, d), lambda i, n: (i // tpb, 0, 0)),
            pl.BlockSpec((1, d), lambda i, n: (0, 0)),
            pl.BlockSpec((d, tn), lambda i, n: (0, n)),
            pl.BlockSpec((d, LANES), lambda i, n: (0, 0)),
            pl.BlockSpec((1, LANES), lambda i, n: (0, 0)),
            pl.BlockSpec((groups, gw, gw), lambda i, n: (0, 0, 0)),
            pl.BlockSpec((1, tn), lambda i, n: (0, 0)),
        ],
        out_specs=[
            pl.BlockSpec((tm, tn), lambda i, n: (i, 0)),
            pl.BlockSpec((tm, tn), lambda i, n: (i, jnp.maximum(n - 1, 0))),
            pl.BlockSpec((tm, LANES), lambda i, n: (i, 0)),
        ],
        out_shape=[
            jax.ShapeDtypeStruct((t, tn), BF16),
            jax.ShapeDtypeStruct((t, 3 * tn), BF16),
            jax.ShapeDtypeStruct((t, LANES), F32),
        ],
        scratch_shapes=[
            pltpu.VMEM((tm, d), BF16),
            pltpu.VMEM((POOL_HALO + tm, tn), F32),
            pltpu.VMEM((1, LANES), F32),
        ],
        compiler_params=_params(("arbitrary", "arbitrary")),
        name="abin",
    )(x, mod, nw, w_main, w_f, b_f, pool_w, pool_scale)


def _fox_kernel(q_ref, k_ref, v_ref, cum_ref, o_ref, qa_ref, ka_ref, va_ref, *, tq, seq):
    h = pl.program_id(1)
    hd = HEAD_DIM
    cum = cum_ref[0]
    lane = lax.broadcasted_iota(jnp.int32, cum.shape, 1)
    f = jnp.sum(jnp.where(lane == h, cum, 0.0), axis=-1, keepdims=True) * LOG2E
    f_hi = f.astype(BF16).astype(F32)
    r1 = f - f_hi
    f_mid = r1.astype(BF16).astype(F32)
    f_lo = r1 - f_mid
    zero = jnp.zeros_like(cum)
    one_or_zero = jnp.where(lane < 6, 1.0, 0.0)
    fq = jnp.where(lane == 0, f_hi, jnp.where(lane == 1, f_mid, jnp.where(lane == 2, f_lo, one_or_zero)))
    fk = jnp.where(lane < 3, 1.0,
                   jnp.where(lane == 3, -f_hi, jnp.where(lane == 4, -f_mid, jnp.where(lane == 5, -f_lo, zero))))
    qa_ref[:, 0:hd] = q_ref[0]
    qa_ref[:, hd:2 * hd] = fq.astype(BF16)
    ka_ref[:, 0:hd] = k_ref[0]
    ka_ref[:, hd:2 * hd] = fk.astype(BF16)
    va_ref[:, 0:hd] = v_ref[0]
    va_ref[:, hd:2 * hd] = jnp.ones((seq, hd), BF16)

    r = lax.broadcasted_iota(jnp.int32, (tq, tq), 0)
    c = lax.broadcasted_iota(jnp.int32, (tq, tq), 1)
    causal = c <= r

    nq = seq // tq
    m = [jnp.full((tq, 1), NEG_BIG, F32) for _ in range(nq)]
    acc = [jnp.zeros((tq, 2 * hd), F32) for _ in range(nq)]
    for j in range(nq):
        ka = ka_ref[j * tq:(j + 1) * tq, :]
        va = va_ref[j * tq:(j + 1) * tq, :]
        for qi in range(j, nq):
            s = lax.dot_general(qa_ref[qi * tq:(qi + 1) * tq, :], ka, (((1,), (1,)), ((), ())),
                                preferred_element_type=F32)
            if qi == j:
                s = jnp.where(causal, s, NEG_BIG)
            m_new = jnp.maximum(m[qi], jnp.max(s, axis=-1, keepdims=True))
            p = jnp.exp2(s - m_new)
            acc[qi] = jnp.exp2(m[qi] - m_new) * acc[qi] + _bdot(p.astype(BF16), va)
            m[qi] = m_new
    for qi in range(nq):
        o_ref[0, qi * tq:(qi + 1) * tq, :] = (acc[qi][:, 0:hd] / acc[qi][:, hd:2 * hd]).astype(BF16)


def _fox(qkv, cum, *, batch, seq, heads, tq=256):
    qkv3 = qkv.reshape(batch, seq, 3 * heads * HEAD_DIM)
    cum3 = cum.reshape(batch, seq, LANES)
    kern = functools.partial(_fox_kernel, tq=tq, seq=seq)
    return pl.pallas_call(
        kern,
        grid=(batch, heads),
        in_specs=[
            pl.BlockSpec((1, seq, HEAD_DIM), lambda b, h: (b, 0, h)),
            pl.BlockSpec((1, seq, HEAD_DIM), lambda b, h: (b, 0, heads + h)),
            pl.BlockSpec((1, seq, HEAD_DIM), lambda b, h: (b, 0, 2 * heads + h)),
            pl.BlockSpec((1, seq, LANES), lambda b, h: (b, 0, 0)),
        ],
        out_specs=pl.BlockSpec((1, seq, HEAD_DIM), lambda b, h: (b, 0, h)),
        out_shape=jax.ShapeDtypeStruct((batch, seq, heads * HEAD_DIM), BF16),
        scratch_shapes=[pltpu.VMEM((seq, 2 * HEAD_DIM), BF16)] * 3,
        compiler_params=_params(("parallel", "parallel")),
        name="fox",
    )(qkv3, qkv3, qkv3, cum3)


def _proj_res_kernel(*refs, n_in, gate_row):
    x_ref, mod_ref = refs[0], refs[1]
    a_refs = refs[2:2 + n_in]
    w_ref, o_ref = refs[2 + n_in], refs[3 + n_in]
    y = None
    row = 0
    for a_ref in a_refs:
        ka = a_ref.shape[1]
        part = _bdot(a_ref[...], w_ref[row:row + ka, :])
        y = part if y is None else y + part
        row += ka
    o_ref[...] = x_ref[...] + mod_ref[0, gate_row:gate_row + 1, :] * y


def _proj_res(x, mod, acts, weight, *, gate_row, seq, tm=256):
    t, d = x.shape
    tpb = seq // tm
    n_in = len(acts)
    kern = functools.partial(_proj_res_kernel, n_in=n_in, gate_row=gate_row)
    in_specs = [
        pl.BlockSpec((tm, d), lambda i: (i, 0)),
        pl.BlockSpec((1, N_MOD, d), lambda i: (i // tpb, 0, 0)),
    ]
    in_specs += [pl.BlockSpec((tm, a.shape[1]), lambda i: (i, 0)) for a in acts]
    in_specs += [pl.BlockSpec(weight.shape, lambda i: (0, 0), pipeline_mode=pl.Buffered(1))]
    return pl.pallas_call(
        kern,
        grid=(t // tm,),
        in_specs=in_specs,
        out_specs=pl.BlockSpec((tm, d), lambda i: (i, 0)),
        out_shape=jax.ShapeDtypeStruct((t, d), F32),
        compiler_params=_params(("parallel",)),
        name="proj_res",
    )(x, mod, *acts, weight)


def _gdnin_kernel(x_ref, mod_ref, nw_ref, w_ref, wg_ref, alog_ref, dtb_ref, cw_ref,
                  o_ref, gate_ref, h_ref, pbuf, carry_ref, *, k0, tm, tpb, n_conv, n_qk, n_q):
    i = pl.program_id(0)
    n = pl.program_id(1)
    first = (i % tpb) == 0

    @pl.when(n == 0)
    def _():
        h = _norm_mod(x_ref[...], nw_ref[...], mod_ref[0, k0:k0 + 1, :], mod_ref[0, k0 + 1:k0 + 2, :])
        h_ref[...] = h.astype(BF16)
        ba = _bdot(h_ref[...], wg_ref[...])
        lane = lax.broadcasted_iota(jnp.int32, ba.shape, 1) % LANES
        g = -jnp.exp(alog_ref[...]) * _softplus(ba + dtb_ref[...])
        gate_ref[...] = jnp.where(lane < GDN_GROUP, g, jax.nn.sigmoid(ba))

        @pl.when(i == 0)
        def _():
            carry_ref[...] = jnp.zeros_like(carry_ref)

    def conv_silu():
        acc = _bdot(h_ref[...], w_ref[...])
        pbuf[0:CONV_HALO, :] = jnp.where(first, 0.0, carry_ref[n])
        pbuf[CONV_HALO:CONV_HALO + tm, :] = acc
        carry_ref[n] = acc[tm - CONV_HALO:tm, :]
        y = cw_ref[CONV_K - 1:CONV_K, :] * acc
        for kk in range(CONV_K - 1):
            lo = CONV_HALO - (CONV_K - 1) + kk
            y = y + cw_ref[kk:kk + 1, :] * pbuf[lo:lo + tm, :]
        return _silu(y)

    @pl.when(n < n_qk)
    def _():
        y = conv_silu()
        qscale = jnp.where(n < n_q, HEAD_DIM ** -0.5, 1.0).astype(F32)
        for hh in range(y.shape[1] // HEAD_DIM):
            seg = y[:, hh * HEAD_DIM:(hh + 1) * HEAD_DIM]
            inv = lax.rsqrt(jnp.sum(seg * seg, axis=-1, keepdims=True) + EPS) * qscale
            o_ref[:, hh * HEAD_DIM:(hh + 1) * HEAD_DIM] = (seg * inv).astype(BF16)

    @pl.when((n >= n_qk) & (n < n_conv))
    def _():
        o_ref[...] = conv_silu().astype(BF16)

    @pl.when(n >= n_conv)
    def _():
        o_ref[...] = _bdot(h_ref[...], w_ref[...]).astype(BF16)


def _gdnin(x, mod, nw, w_main, w_gate, alog_l, dtb_l, conv_w, *, k0, seq, qk_dim, conv_ch, n_main,
           tm=512, tn=1024):
    t, d = x.shape
    n_total = n_main
    tpb = seq // tm
    n_conv = conv_ch // tn
    n_qk = 2 * qk_dim // tn
    n_q = qk_dim // tn
    gl = w_gate.shape[1]
    kern = functools.partial(_gdnin_kernel, k0=k0, tm=tm, tpb=tpb, n_conv=n_conv, n_qk=n_qk, n_q=n_q)
    return pl.pallas_call(
        kern,
        grid=(t // tm, n_total // tn),
        in_specs=[
            pl.BlockSpec((tm, d), lambda i, n: (i, 0)),
            pl.BlockSpec((1, N_MOD, d), lambda i, n: (i // tpb, 0, 0)),
            pl.BlockSpec((1, d), lambda i, n: (0, 0)),
            pl.BlockSpec((d, tn), lambda i, n: (0, n)),
            pl.BlockSpec((d, gl), lambda i, n: (0, 0)),
            pl.BlockSpec((1, gl), lambda i, n: (0, 0)),
            pl.BlockSpec((1, gl), lambda i, n: (0, 0)),
            pl.BlockSpec((CONV_K, tn), lambda i, n: (0, jnp.minimum(n, n_conv - 1))),
        ],
        out_specs=[
            pl.BlockSpec((tm, tn), lambda i, n: (i, n)),
            pl.BlockSpec((tm, gl), lambda i, n: (i, 0)),
        ],
        out_shape=[
            jax.ShapeDtypeStruct((t, n_total), BF16),
            jax.ShapeDtypeStruct((t, gl), F32),
        ],
        scratch_shapes=[
            pltpu.VMEM((tm, d), BF16),
            pltpu.VMEM((CONV_HALO + tm, tn), F32),
            pltpu.VMEM((n_conv, CONV_HALO, tn), F32),
        ],
        compiler_params=_params(("arbitrary", "arbitrary")),
        name="gdnin",
    )(x, mod, nw, w_main, w_gate, alog_l, dtb_l, conv_w)


def _gdn_kernel(q_ref, k_ref, v_ref, z_ref, gt_ref, nw_ref, o_ref, state_ref, *, n_chunks):
    @pl.when(pl.program_id(2) == 0)
    def _():
        state_ref[...] = jnp.zeros_like(state_ref)

    c = CHUNK
    hd = HEAD_DIM
    row = lax.broadcasted_iota(jnp.int32, (c, 2 * c), 0)
    lane = lax.broadcasted_iota(jnp.int32, (c, 2 * c), 1)
    col = lane % c
    left = lane < c
    tril = row >= col
    strict = row > col
    eye2 = (row == col).astype(F32)
    br = lax.broadcasted_iota(jnp.int32, (2 * c, 2 * c), 0) // c
    bc = lax.broadcasted_iota(jnp.int32, (2 * c, 2 * c), 1) // c
    same_block = br == bc

    def blockdiag(p):
        return jnp.where(same_block, jnp.concatenate([p, p], axis=0), 0.0)

    def split(v):
        head = v.astype(BF16)
        return head, (v - head.astype(F32)).astype(BF16)

    nw = nw_ref[...]
    zeros_b = jnp.zeros((c, hd), BF16)
    n_pairs = GDN_GROUP // 2
    items = [(cc, jq) for cc in range(n_chunks) for jq in range(n_pairs)]

    ts = n_chunks * c
    raw = gt_ref[0]
    rr = lax.broadcasted_iota(jnp.int32, (ts, ts), 0)
    rc = lax.broadcasted_iota(jnp.int32, (ts, ts), 1)
    tri = jnp.where((rr >= rc) & ((rr // c) == (rc // c)), 1.0, 0.0).astype(BF16)
    gates_all = jnp.where(lax.broadcasted_iota(jnp.int32, raw.shape, 1) < GDN_GROUP, _tri_cumsum(tri, raw), raw)

    chunk = []
    for cc in range(n_chunks):
        rows = slice(cc * c, (cc + 1) * c)
        gates = gates_all[rows, :]
        g_last = gates[c - 1:c, :]
        chunk.append(dict(
            rows=rows, gates=gates,
            gates_t=jnp.concatenate([gates, gates], axis=0).T,
            exp_g=jnp.exp(gates),
            exp_rest=jnp.exp(g_last - gates),
            exp_last=jnp.exp(g_last)))

    it = {}
    for key in items:
        cc, jq = key
        ck = chunk[cc]
        gates, gates_t, rows = ck["gates"], ck["gates_t"], ck["rows"]
        ha, hb = 2 * jq, 2 * jq + 1
        q = q_ref[0, rows, jq * hd:(jq + 1) * hd]
        k = k_ref[0, rows, jq * hd:(jq + 1) * hd]
        kq = lax.dot_general(jnp.concatenate([k, q], axis=0), jnp.concatenate([k, k], axis=0),
                             (((1,), (1,)), ((), ())), preferred_element_type=F32)
        g_col = jnp.where(left, gates[:, ha:ha + 1], gates[:, hb:hb + 1])
        g_row = jnp.where(left[0:1], gates_t[ha:ha + 1, :], gates_t[hb:hb + 1, :])
        beta_a = gates[:, GDN_GROUP + ha:GDN_GROUP + ha + 1]
        beta_b = gates[:, GDN_GROUP + hb:GDN_GROUP + hb + 1]
        decay = jnp.where(tril, jnp.exp(g_col - g_row), 0.0)
        a_mat = jnp.where(strict, jnp.where(left, beta_a, beta_b) * kq[:c] * decay, 0.0)
        it[key] = dict(q=q, k=k, attn=(kq[c:] * decay).astype(BF16), a=a_mat, p=-a_mat, x=eye2 - a_mat,
                       heads=((ha, beta_a), (hb, beta_b)))

    for _ in range(5):
        for key in items:
            d = it[key]
            d["p"] = _bdot(d["p"].astype(BF16), blockdiag(d["p"]).astype(BF16))
        for key in items:
            d = it[key]
            d["x"] = d["x"] + _bdot(d["x"].astype(BF16), blockdiag(d["p"]).astype(BF16))

    for key in items:
        d = it[key]
        a_hi, a_lo = split(eye2 + d["a"])
        x_hi, x_lo = split(blockdiag(d["x"]))
        prod = _bdot(jnp.concatenate([a_hi, a_lo, a_hi], axis=1), jnp.concatenate([x_hi, x_hi, x_lo], axis=0))
        d["r"] = eye2 - prod
    for key in items:
        d = it[key]
        d["x"] = d["x"] + _bdot(d["x"].astype(BF16), blockdiag(d["r"]).astype(BF16))

    for key in items:
        d = it[key]
        ck = chunk[key[0]]
        kf = d["k"].astype(F32)
        vbs, kbs = [], []
        for h, beta in d["heads"]:
            vf = v_ref[0, ck["rows"], h * hd:(h + 1) * hd].astype(F32)
            vbs.append((vf * beta).astype(BF16))
            kbs.append((kf * (beta * ck["exp_g"][:, h:h + 1])).astype(BF16))
        rhs = jnp.concatenate([
            jnp.concatenate([vbs[0], zeros_b, kbs[0], zeros_b], axis=1),
            jnp.concatenate([zeros_b, vbs[1], zeros_b, kbs[1]], axis=1)], axis=0)
        d["uw"] = _bdot(d["x"].astype(BF16), rhs)
        d["kg"] = jnp.concatenate([(kf * ck["exp_rest"][:, h:h + 1]).astype(BF16) for h, _ in d["heads"]], axis=0)
        qf = d["q"].astype(F32)
        d["qg"] = [(qf * ck["exp_g"][:, h:h + 1]).astype(BF16) for h, _ in d["heads"]]

    states = [state_ref[h] for h in range(GDN_GROUP)]
    for cc in range(n_chunks):
        ck = chunk[cc]
        wqs = {}
        for jq in range(n_pairs):
            d = it[(cc, jq)]
            for r, (h, _) in enumerate(d["heads"]):
                w_r = d["uw"][:, (2 + r) * hd:(3 + r) * hd].astype(BF16)
                wqs[h] = _bdot(jnp.concatenate([w_r, d["qg"][r]], axis=0), states[h].astype(BF16))
        outs = {}
        for jq in range(n_pairs):
            d = it[(cc, jq)]
            (ha, _), (hb, _) = d["heads"]
            vna = (d["uw"][:, 0:hd] - wqs[ha][:c]).astype(BF16)
            vnb = (d["uw"][:, hd:2 * hd] - wqs[hb][:c]).astype(BF16)
            vn = jnp.concatenate([jnp.concatenate([vna, zeros_b], axis=1),
                                  jnp.concatenate([zeros_b, vnb], axis=1)], axis=0)
            o_intra = _bdot(d["attn"], vn)
            ds = lax.dot_general(d["kg"], vn, (((0,), (0,)), ((), ())), preferred_element_type=F32)
            for r, h in enumerate((ha, hb)):
                states[h] = states[h] * ck["exp_last"][0:1, h:h + 1] + ds[:, r * hd:(r + 1) * hd]
                outs[h] = wqs[h][c:] + o_intra[:, r * hd:(r + 1) * hd]
        for h in range(GDN_GROUP):
            o = outs[h]
            on = o * lax.rsqrt(jnp.mean(o * o, axis=-1, keepdims=True) + EPS) * nw
            zf = z_ref[0, ck["rows"], h * hd:(h + 1) * hd].astype(F32)
            o_ref[0, ck["rows"], h * hd:(h + 1) * hd] = (on * _silu(zf)).astype(BF16)

    for h in range(GDN_GROUP):
        state_ref[h] = states[h]


def _gdn(qkvz, gates, norm_w, *, batch, seq, qk_dim, v_dim, n_chunks=8):
    total = qkvz.shape[1]
    x3 = qkvz.reshape(batch, seq, total)
    g3 = gates.reshape(batch, seq, gates.shape[1])
    hd = HEAD_DIM
    qw = GDN_GROUP // 2 * hd
    vw = GDN_GROUP * hd
    n_groups = v_dim // vw
    k_off = qk_dim // qw
    v_off = 2 * qk_dim // vw
    z_off = (2 * qk_dim + v_dim) // vw
    ts = n_chunks * CHUNK
    return pl.pallas_call(
        functools.partial(_gdn_kernel, n_chunks=n_chunks),
        grid=(batch, n_groups, seq // ts),
        in_specs=[
            pl.BlockSpec((1, ts, qw), lambda b, g, c: (b, c, g)),
            pl.BlockSpec((1, ts, qw), lambda b, g, c: (b, c, k_off + g)),
            pl.BlockSpec((1, ts, vw), lambda b, g, c: (b, c, v_off + g)),
            pl.BlockSpec((1, ts, vw), lambda b, g, c: (b, c, z_off + g)),
            pl.BlockSpec((1, ts, LANES), lambda b, g, c: (b, c, g)),
            pl.BlockSpec((1, hd), lambda b, g, c: (0, 0)),
        ],
        out_specs=pl.BlockSpec((1, ts, vw), lambda b, g, c: (b, c, g)),
        out_shape=jax.ShapeDtypeStruct((batch, seq, v_dim), BF16),
        scratch_shapes=[pltpu.VMEM((GDN_GROUP, hd, hd), F32)],
        compiler_params=_params(("parallel", "parallel", "arbitrary")),
        name="gdn",
    )(x3, x3, x3, x3, g3, norm_w)


def _grouped_gate_layout(w_b, w_a, a_log, dt_bias):
    d, hv = w_a.shape
    ng = hv // GDN_GROUP
    pad = LANES - 2 * GDN_GROUP
    wa = w_a.reshape(d, ng, GDN_GROUP)
    wb = w_b.reshape(d, ng, GDN_GROUP)
    w = jnp.concatenate([wa, wb, jnp.zeros((d, ng, pad), w_a.dtype)], axis=-1).reshape(d, ng * LANES)
    zg = jnp.zeros((ng, LANES - GDN_GROUP), F32)
    alog = jnp.concatenate([a_log.reshape(ng, GDN_GROUP).astype(F32), zg], axis=-1).reshape(1, ng * LANES)
    dtb = jnp.concatenate([dt_bias.reshape(ng, GDN_GROUP).astype(F32), zg], axis=-1).reshape(1, ng * LANES)
    return w.astype(BF16), alog, dtb


def kernel(x, c, ada_w, ada_b, norm_w, ffn_w_gu, ffn_w_down, ab_w_in, ab_b_f, pool_w, pool_scale, ab_w_out,
           gdn_w_in, gdn_conv_w, gdn_a_log, gdn_dt_bias, gdn_norm_w, gdn_w_out, final_norm_w):
    batch, seq, d = x.shape
    depth = ada_w.shape[0]
    t = batch * seq

    mod_all = _adaln(c, ada_w, ada_b).reshape(depth, batch, N_MOD, d)
    w_gu = ffn_w_gu[0, 0].astype(BF16)
    w_down = ffn_w_down[0, 0].astype(BF16)

    def ffn(xf, mod, w_gu, w_down, layer, half):
        last = layer == depth - 1 and half == 1
        nxt = None if last else (ffn_w_gu, ffn_w_down, layer + half, 1 - half)
        out = _ffn(xf, mod, nw4, w_gu, w_down, fw, layer=layer, half=half, seq=seq, final=last, next_f32=nxt)
        return (out, None, None) if last else out
    fw = final_norm_w.reshape(1, d)
    xf = x.reshape(t, d)

    nw4 = norm_w.reshape(depth, 3, 1, d)
    for layer in range(depth):
        mod = mod_all[layer]
        e = layer // 2
        nws = nw4[layer]
        xf, w_gu, w_down = ffn(xf, mod, w_gu, w_down, layer, 0)

        if layer % 2 == 0:
            pool_width = pool_scale.shape[1]
            fox_heads = ab_b_f.shape[1]
            fox_width = fox_heads * HEAD_DIM
            n_main = pool_width + 3 * fox_width
            w_in = ab_w_in[e]
            w_f = jnp.pad(w_in[:, n_main:], ((0, 0), (0, LANES - fox_heads))).astype(BF16)
            b_f = jnp.pad(ab_b_f[e], (0, LANES - fox_heads)).reshape(1, LANES)
            pool_out, qkv, cum = _abin(xf, mod, nws[1], w_in.astype(BF16), w_f, b_f, pool_w[e].astype(BF16),
                                       pool_scale[e].reshape(1, pool_width), k0=3, seq=seq)
            attn = _fox(qkv, cum, batch=batch, seq=seq, heads=fox_heads)
            w_out = ab_w_out[e].astype(BF16)
            xf = _proj_res(xf, mod, [pool_out, attn.reshape(t, fox_width)], w_out, gate_row=5, seq=seq)
        else:
            v_heads = gdn_a_log.shape[1]
            v_dim = v_heads * HEAD_DIM
            conv_ch = gdn_conv_w.shape[2]
            qk_dim = (conv_ch - v_dim) // 2
            n_main = conv_ch + v_dim
            w_in = gdn_w_in[e]
            w_gate, alog_l, dtb_l = _grouped_gate_layout(
                w_in[:, n_main:n_main + v_heads], w_in[:, n_main + v_heads:], gdn_a_log[e], gdn_dt_bias[e])
            qkvz, gates = _gdnin(xf, mod, nws[1], w_in.astype(BF16), w_gate, alog_l, dtb_l,
                                 gdn_conv_w[e], k0=3, seq=seq, qk_dim=qk_dim, conv_ch=conv_ch, n_main=n_main)
            o = _gdn(qkvz, gates, gdn_norm_w[e].reshape(1, HEAD_DIM), batch=batch, seq=seq,
                     qk_dim=qk_dim, v_dim=v_dim)
            xf = _proj_res(xf, mod, [o.reshape(t, v_dim)], gdn_w_out[e].astype(BF16), gate_row=5, seq=seq)

        xf, w_gu, w_down = ffn(xf, mod, w_gu, w_down, layer, 1)
    return xf.reshape(batch, seq, d)
```

```python
import functools

import jax
import jax.numpy as jnp
from jax import lax
from jax.experimental import pallas as pl
from jax.experimental.pallas import tpu as pltpu

F32 = jnp.float32
BF16 = jnp.bfloat16
EPS = 1e-6

LANES = 128
N_MOD = 9
POOL_WINDOWS = (2, 4, 8, 16)
POOL_HALO = 16
HEAD_DIM = 128
CONV_K = 4
CONV_HALO = 8
CHUNK = 64
GDN_GROUP = 16
NEG_BIG = -1e30
LOG2E = 1.4426950408889634
VMEM_LIMIT = 56 * 1024 * 1024
VMEM_LIMIT_BIG = 61 * 1024 * 1024 + 256 * 1024


def _silu(x):
    return x * jax.nn.sigmoid(x)


def _softplus(x):
    return jnp.maximum(x, 0.0) + jnp.log1p(jnp.exp(-jnp.abs(x)))


def _bdot(a, b):
    return jnp.dot(a, b, preferred_element_type=F32)


def _tri_cumsum(tri, x):
    x1 = x.astype(BF16)
    r1 = x - x1.astype(F32)
    x2 = r1.astype(BF16)
    x3 = (r1 - x2.astype(F32)).astype(BF16)
    s = _bdot(tri, jnp.concatenate([x1, x2, x3], axis=1))
    return s[:, 0:LANES] + s[:, LANES:2 * LANES] + s[:, 2 * LANES:3 * LANES]


def _norm_mod(x, nw, shift, scale):
    y = x * lax.rsqrt(jnp.mean(x * x, axis=-1, keepdims=True) + EPS) * nw
    return y * (1.0 + scale) + shift


def _params(sem, vmem=VMEM_LIMIT):
    return pltpu.CompilerParams(dimension_semantics=sem, vmem_limit_bytes=vmem)


def _adaln_kernel(c_ref, w_ref, b_ref, o_ref):
    ca = _silu(c_ref[...]).astype(BF16)
    o_ref[0] = _bdot(ca, w_ref[0].astype(BF16)) + b_ref[0]


def _adaln(c, ada_w, ada_b):
    depth, d, n = ada_w.shape
    b = c.shape[0]
    tn = 2048
    return pl.pallas_call(
        _adaln_kernel,
        grid=(depth, n // tn),
        in_specs=[
            pl.BlockSpec((b, d), lambda l, j: (0, 0)),
            pl.BlockSpec((1, d, tn), lambda l, j: (l, 0, j)),
            pl.BlockSpec((1, 1, tn), lambda l, j: (l, 0, j)),
        ],
        out_specs=pl.BlockSpec((1, b, tn), lambda l, j: (l, 0, j)),
        out_shape=jax.ShapeDtypeStruct((depth, b, n), F32),
        compiler_params=_params(("parallel", "parallel")),
        name="adaln",
    )(c, ada_w, ada_b.reshape(depth, 1, n))


def _ffn_kernel(*refs, k0, n_ff, final, cast_next):
    x_ref, mod_ref, nw_ref, wg_ref, wu_ref, wd_ref, fw_ref = refs[:7]
    if cast_next:
        gu_src, down_src, o_ref, gu_dst, down_dst, h_ref = refs[7:]
        gu_dst[...] = gu_src[...].astype(BF16)
        down_dst[...] = down_src[...].astype(BF16)
    else:
        o_ref, h_ref = refs[7:]
    j = pl.program_id(1)

    @pl.when(j == 0)
    def _():
        h = _norm_mod(x_ref[...], nw_ref[...], mod_ref[0, k0:k0 + 1, :], mod_ref[0, k0 + 1:k0 + 2, :])
        h_ref[...] = h.astype(BF16)
        o_ref[...] = x_ref[...]

    half_gate = 0.5 * mod_ref[0, k0 + 2:k0 + 3, :]
    rows = h_ref.shape[0] // 2
    for r in range(2):
        sl = slice(r * rows, (r + 1) * rows)
        h = h_ref[sl, :]
        g = _bdot(h, wg_ref[...])
        u = _bdot(h, wu_ref[...])
        o_ref[sl, :] += half_gate * _bdot((_silu(g) * u).astype(BF16), wd_ref[...])

    if final:
        @pl.when(j == n_ff - 1)
        def _():
            xn = o_ref[...]
            o_ref[...] = xn * lax.rsqrt(jnp.mean(xn * xn, axis=-1, keepdims=True) + EPS) * fw_ref[...]


def _ffn(x, mod, nw, w_gu, w_down, fw, *, layer, half, seq, final, next_f32=None, tm=1024, tf=512):
    t, d = x.shape
    dff = w_down.shape[0]
    n_i, n_ff = t // tm, dff // tf
    tpb = seq // tm
    k0 = 6 * half
    cast_next = next_f32 is not None
    kern = functools.partial(_ffn_kernel, k0=k0, n_ff=n_ff, final=final, cast_next=cast_next)
    in_specs = [
        pl.BlockSpec((tm, d), lambda i, j: (i, 0)),
        pl.BlockSpec((1, N_MOD, d), lambda i, j: (i // tpb, 0, 0)),
        pl.BlockSpec((None, None, 1, d), lambda i, j: (layer, 2 * half, 0, 0)),
        pl.BlockSpec((d, tf), lambda i, j: (0, j)),
        pl.BlockSpec((d, tf), lambda i, j: (0, j + n_ff)),
        pl.BlockSpec((tf, d), lambda i, j: (j, 0)),
        pl.BlockSpec((1, d), lambda i, j: (0, 0)),
    ]
    out_specs = [pl.BlockSpec((tm, d), lambda i, j: (i, 0))]
    out_shape = [jax.ShapeDtypeStruct((t, d), F32)]
    args = [x, mod, nw, w_gu, w_gu, w_down, fw]
    if cast_next:
        gu32, down32, nl, nh = next_f32
        gu_blk = (d // n_i, 2 * dff // n_ff)
        down_blk = (dff // n_ff, d // n_i)
        in_specs += [pl.BlockSpec((None, None) + gu_blk, lambda i, j: (nl, nh, i, j)),
                     pl.BlockSpec((None, None) + down_blk, lambda i, j: (nl, nh, j, i))]
        out_specs += [pl.BlockSpec(gu_blk, lambda i, j: (i, j)), pl.BlockSpec(down_blk, lambda i, j: (j, i))]
        out_shape += [jax.ShapeDtypeStruct((d, 2 * dff), BF16), jax.ShapeDtypeStruct((dff, d), BF16)]
        args += [gu32, down32]
    outs = pl.pallas_call(
        kern,
        grid=(n_i, n_ff),
        in_specs=in_specs,
        out_specs=out_specs,
        out_shape=out_shape,
        scratch_shapes=[pltpu.VMEM((tm, d), BF16)],
        compiler_params=_params(("parallel", "arbitrary"), vmem=VMEM_LIMIT_BIG),
        name="ffn",
    )(*args)
    return outs if cast_next else outs[0]


def _abin_kernel(x_ref, mod_ref, nw_ref, w_ref, wf_ref, bf_ref, pw_ref, ps_ref,
                 pool_ref, qkv_ref, cum_ref, h_ref, ubuf, fcarry, *, k0, tm, tpb, gw, tn):
    i = pl.program_id(0)
    first = (i % tpb) == 0

    h = _norm_mod(x_ref[...], nw_ref[...], mod_ref[0, k0:k0 + 1, :], mod_ref[0, k0 + 1:k0 + 2, :])
    h_ref[...] = h.astype(BF16)
    hb = h_ref[...]

    for part in range(3):
        acc = _bdot(hb, w_ref[:, (part + 1) * tn:(part + 2) * tn])
        if part == 0:
            acc = acc * (LOG2E * HEAD_DIM ** -0.5)
        qkv_ref[:, part * tn:(part + 1) * tn] = acc.astype(BF16)

    f = _bdot(hb, wf_ref[...]) + bf_ref[...]
    log_f = jnp.minimum(f, 0.0) - jnp.log1p(jnp.exp(-jnp.abs(f)))
    r = lax.broadcasted_iota(jnp.int32, (tm, tm), 0)
    c = lax.broadcasted_iota(jnp.int32, (tm, tm), 1)
    tri = jnp.where(r >= c, 1.0, 0.0).astype(BF16)

    @pl.when(first)
    def _():
        fcarry[...] = jnp.zeros_like(fcarry)
        ubuf[0:POOL_HALO, :] = jnp.zeros((POOL_HALO, ubuf.shape[1]), F32)

    cum = _tri_cumsum(tri, log_f) + fcarry[...]
    cum_ref[...] = cum
    fcarry[...] = cum[tm - 1:tm, :]

    u = _bdot(hb, w_ref[:, 0:tn])
    ubuf[POOL_HALO:POOL_HALO + tm, :] = u
    pos = (i % tpb) * tm + lax.broadcasted_iota(jnp.int32, (tm, 1), 0)
    for g, win in enumerate(POOL_WINDOWS):
        sl = slice(g * gw, (g + 1) * gw)
        acc = u[:, sl]
        for s in range(1, win):
            acc = acc + ubuf[POOL_HALO - s:POOL_HALO - s + tm, sl]
        cnt = jnp.minimum(pos + 1, win).astype(F32)
        dlt = acc / cnt - u[:, sl]
        y = _bdot(dlt.astype(BF16), pw_ref[g]) * ps_ref[:, sl]
        pool_ref[:, sl] = y.astype(BF16)
    ubuf[0:POOL_HALO, :] = ubuf[tm:tm + POOL_HALO, :]


def _abin(x, mod, nw, w_main, w_f, b_f, pool_w, pool_scale, *, k0, seq, tm=512):
    t, d = x.shape
    tn = pool_scale.shape[1]
    tpb = seq // tm
    groups = pool_w.shape[0]
    gw = tn // groups
    kern = functools.partial(_abin_kernel, k0=k0, tm=tm, tpb=tpb, gw=gw, tn=tn)
    return pl.pallas_call(
        kern,
        grid=(t // tm,),
        in_specs=[
            pl.BlockSpec((tm, d), lambda i: (i, 0)),
            pl.BlockSpec((1, N_MOD, d), lambda i: (i // tpb, 0, 0)),
            pl.BlockSpec((1, d), lambda i: (0, 0)),
            pl.BlockSpec(w_main.shape, lambda i: (0, 0), pipeline_mode=pl.Buffered(1)),
            pl.BlockSpec((d, LANES), lambda i: (0, 0)),
            pl.BlockSpec((1, LANES), lambda i: (0, 0)),
            pl.BlockSpec((groups, gw, gw), lambda i: (0, 0, 0)),
            pl.BlockSpec((1, tn), lambda i: (0, 0)),
        ],
        out_specs=[
            pl.BlockSpec((tm, tn), lambda i: (i, 0)),
            pl.BlockSpec((tm, 3 * tn), lambda i: (i, 0)),
            pl.BlockSpec((tm, LANES), lambda i: (i, 0)),
        ],
        out_shape=[
            jax.ShapeDtypeStruct((t, tn), BF16),
            jax.ShapeDtypeStruct((t, 3 * tn), BF16),
            jax.ShapeDtypeStruct((t, LANES), F32),
        ],
        scratch_shapes=[
            pltpu.VMEM((tm, d), BF16),
            pltpu.VMEM((POOL_HALO + tm, tn), F32),
            pltpu.VMEM((1, LANES), F32),
        ],
        compiler_params=_params(("arbitrary",)),
        name="abin",
    )(x, mod, nw, w_main, w_f, b_f, pool_w, pool_scale)


def _fox_kernel(q_ref, k_ref, v_ref, cum_ref, o_ref, qa_ref, ka_ref, va_ref, *, tq, seq):
    h = pl.program_id(1)
    hd = HEAD_DIM
    cum = cum_ref[0]
    lane = lax.broadcasted_iota(jnp.int32, cum.shape, 1)
    f = jnp.sum(jnp.where(lane == h, cum, 0.0), axis=-1, keepdims=True) * LOG2E
    f_hi = f.astype(BF16).astype(F32)
    r1 = f - f_hi
    f_mid = r1.astype(BF16).astype(F32)
    f_lo = r1 - f_mid
    zero = jnp.zeros_like(cum)
    one_or_zero = jnp.where(lane < 6, 1.0, 0.0)
    fq = jnp.where(lane == 0, f_hi, jnp.where(lane == 1, f_mid, jnp.where(lane == 2, f_lo, one_or_zero)))
    fk = jnp.where(lane < 3, 1.0,
                   jnp.where(lane == 3, -f_hi, jnp.where(lane == 4, -f_mid, jnp.where(lane == 5, -f_lo, zero))))
    qa_ref[:, 0:hd] = q_ref[0]
    qa_ref[:, hd:2 * hd] = fq.astype(BF16)
    ka_ref[:, 0:hd] = k_ref[0]
    ka_ref[:, hd:2 * hd] = fk.astype(BF16)
    va_ref[:, 0:hd] = v_ref[0]
    va_ref[:, hd:2 * hd] = jnp.ones((seq, hd), BF16)

    r = lax.broadcasted_iota(jnp.int32, (tq, tq), 0)
    c = lax.broadcasted_iota(jnp.int32, (tq, tq), 1)
    causal = c <= r

    nq = seq // tq
    m = [jnp.full((tq, 1), NEG_BIG, F32) for _ in range(nq)]
    acc = [jnp.zeros((tq, 2 * hd), F32) for _ in range(nq)]
    for j in range(nq):
        ka = ka_ref[j * tq:(j + 1) * tq, :]
        va = va_ref[j * tq:(j + 1) * tq, :]
        for qi in range(j, nq):
            s = lax.dot_general(qa_ref[qi * tq:(qi + 1) * tq, :], ka, (((1,), (1,)), ((), ())),
                                preferred_element_type=F32)
            if qi == j:
                s = jnp.where(causal, s, NEG_BIG)
            m_new = jnp.maximum(m[qi], jnp.max(s, axis=-1, keepdims=True))
            p = jnp.exp2(s - m_new)
            acc[qi] = jnp.exp2(m[qi] - m_new) * acc[qi] + _bdot(p.astype(BF16), va)
            m[qi] = m_new
    for qi in range(nq):
        o_ref[0, qi * tq:(qi + 1) * tq, :] = (acc[qi][:, 0:hd] / acc[qi][:, hd:2 * hd]).astype(BF16)


def _fox(qkv, cum, *, batch, seq, heads, tq=256):
    qkv3 = qkv.reshape(batch, seq, 3 * heads * HEAD_DIM)
    cum3 = cum.reshape(batch, seq, LANES)
    kern = functools.partial(_fox_kernel, tq=tq, seq=seq)
    return pl.pallas_call(
        kern,
        grid=(batch, heads),
        in_specs=[
            pl.BlockSpec((1, seq, HEAD_DIM), lambda b, h: (b, 0, h)),
            pl.BlockSpec((1, seq, HEAD_DIM), lambda b, h: (b, 0, heads + h)),
            pl.BlockSpec((1, seq, HEAD_DIM), lambda b, h: (b, 0, 2 * heads + h)),
            pl.BlockSpec((1, seq, LANES), lambda b, h: (b, 0, 0)),
        ],
        out_specs=pl.BlockSpec((1, seq, HEAD_DIM), lambda b, h: (b, 0, h)),
        out_shape=jax.ShapeDtypeStruct((batch, seq, heads * HEAD_DIM), BF16),
        scratch_shapes=[pltpu.VMEM((seq, 2 * HEAD_DIM), BF16)] * 3,
        compiler_params=_params(("parallel", "parallel")),
        name="fox",
    )(qkv3, qkv3, qkv3, cum3)


def _proj_res_kernel(*refs, n_in, gate_row):
    x_ref, mod_ref = refs[0], refs[1]
    a_refs = refs[2:2 + n_in]
    w_ref, o_ref = refs[2 + n_in], refs[3 + n_in]
    y = None
    row = 0
    for a_ref in a_refs:
        ka = a_ref.shape[1]
        part = _bdot(a_ref[...], w_ref[row:row + ka, :])
        y = part if y is None else y + part
        row += ka
    o_ref[...] = x_ref[...] + mod_ref[0, gate_row:gate_row + 1, :] * y


def _proj_res(x, mod, acts, weight, *, gate_row, seq, tm=512):
    t, d = x.shape
    tpb = seq // tm
    n_in = len(acts)
    kern = functools.partial(_proj_res_kernel, n_in=n_in, gate_row=gate_row)
    in_specs = [
        pl.BlockSpec((tm, d), lambda i: (i, 0)),
        pl.BlockSpec((1, N_MOD, d), lambda i: (i // tpb, 0, 0)),
    ]
    in_specs += [pl.BlockSpec((tm, a.shape[1]), lambda i: (i, 0)) for a in acts]
    in_specs += [pl.BlockSpec(weight.shape, lambda i: (0, 0), pipeline_mode=pl.Buffered(1))]
    return pl.pallas_call(
        kern,
        grid=(t // tm,),
        in_specs=in_specs,
        out_specs=pl.BlockSpec((tm, d), lambda i: (i, 0)),
        out_shape=jax.ShapeDtypeStruct((t, d), F32),
        compiler_params=_params(("parallel",)),
        name="proj_res",
    )(x, mod, *acts, weight)


def _gdnin_kernel(x_ref, mod_ref, nw_ref, w_ref, wg_ref, alog_ref, dtb_ref, cw_ref,
                  o_ref, gate_ref, h_ref, pbuf, carry_ref, *, k0, tm, tpb, n_conv, n_qk, n_q):
    i = pl.program_id(0)
    n = pl.program_id(1)
    first = (i % tpb) == 0

    @pl.when(n == 0)
    def _():
        h = _norm_mod(x_ref[...], nw_ref[...], mod_ref[0, k0:k0 + 1, :], mod_ref[0, k0 + 1:k0 + 2, :])
        h_ref[...] = h.astype(BF16)
        ba = _bdot(h_ref[...], wg_ref[...])
        lane = lax.broadcasted_iota(jnp.int32, ba.shape, 1) % LANES
        g = -jnp.exp(alog_ref[...]) * _softplus(ba + dtb_ref[...])
        gate_ref[...] = jnp.where(lane < GDN_GROUP, g, jax.nn.sigmoid(ba))

        @pl.when(i == 0)
        def _():
            carry_ref[...] = jnp.zeros_like(carry_ref)

    def conv_silu():
        acc = _bdot(h_ref[...], w_ref[...])
        pbuf[0:CONV_HALO, :] = jnp.where(first, 0.0, carry_ref[n])
        pbuf[CONV_HALO:CONV_HALO + tm, :] = acc
        carry_ref[n] = acc[tm - CONV_HALO:tm, :]
        y = cw_ref[CONV_K - 1:CONV_K, :] * acc
        for kk in range(CONV_K - 1):
            lo = CONV_HALO - (CONV_K - 1) + kk
            y = y + cw_ref[kk:kk + 1, :] * pbuf[lo:lo + tm, :]
        return _silu(y)

    @pl.when(n < n_qk)
    def _():
        y = conv_silu()
        qscale = jnp.where(n < n_q, HEAD_DIM ** -0.5, 1.0).astype(F32)
        for hh in range(y.shape[1] // HEAD_DIM):
            seg = y[:, hh * HEAD_DIM:(hh + 1) * HEAD_DIM]
            inv = lax.rsqrt(jnp.sum(seg * seg, axis=-1, keepdims=True) + EPS) * qscale
            o_ref[:, hh * HEAD_DIM:(hh + 1) * HEAD_DIM] = (seg * inv).astype(BF16)

    @pl.when((n >= n_qk) & (n < n_conv))
    def _():
        o_ref[...] = conv_silu().astype(BF16)

    @pl.when(n >= n_conv)
    def _():
        o_ref[...] = _bdot(h_ref[...], w_ref[...]).astype(BF16)


def _gdnin(x, mod, nw, w_main, w_gate, alog_l, dtb_l, conv_w, *, k0, seq, qk_dim, conv_ch, n_main,
           tm=512, tn=2048):
    t, d = x.shape
    n_total = n_main
    tpb = seq // tm
    n_conv = conv_ch // tn
    n_qk = 2 * qk_dim // tn
    n_q = qk_dim // tn
    gl = w_gate.shape[1]
    kern = functools.partial(_gdnin_kernel, k0=k0, tm=tm, tpb=tpb, n_conv=n_conv, n_qk=n_qk, n_q=n_q)
    return pl.pallas_call(
        kern,
        grid=(t // tm, n_total // tn),
        in_specs=[
            pl.BlockSpec((tm, d), lambda i, n: (i, 0)),
            pl.BlockSpec((1, N_MOD, d), lambda i, n: (i // tpb, 0, 0)),
            pl.BlockSpec((1, d), lambda i, n: (0, 0)),
            pl.BlockSpec((d, tn), lambda i, n: (0, n)),
            pl.BlockSpec((d, gl), lambda i, n: (0, 0)),
            pl.BlockSpec((1, gl), lambda i, n: (0, 0)),
            pl.BlockSpec((1, gl), lambda i, n: (0, 0)),
            pl.BlockSpec((CONV_K, tn), lambda i, n: (0, jnp.minimum(n, n_conv - 1))),
        ],
        out_specs=[
            pl.BlockSpec((tm, tn), lambda i, n: (i, n)),
            pl.BlockSpec((tm, gl), lambda i, n: (i, 0)),
        ],
        out_shape=[
            jax.ShapeDtypeStruct((t, n_total), BF16),
            jax.ShapeDtypeStruct((t, gl), F32),
        ],
        scratch_shapes=[
            pltpu.VMEM((tm, d), BF16),
            pltpu.VMEM((CONV_HALO + tm, tn), F32),
            pltpu.VMEM((n_conv, CONV_HALO, tn), F32),
        ],
        compiler_params=_params(("arbitrary", "arbitrary")),
        name="gdnin",
    )(x, mod, nw, w_main, w_gate, alog_l, dtb_l, conv_w)


def _gdn_kernel(q_ref, k_ref, v_ref, z_ref, gt_ref, nw_ref, o_ref, state_ref, *, n_chunks):
    @pl.when(pl.program_id(2) == 0)
    def _():
        state_ref[...] = jnp.zeros_like(state_ref)

    c = CHUNK
    hd = HEAD_DIM
    row = lax.broadcasted_iota(jnp.int32, (c, 2 * c), 0)
    lane = lax.broadcasted_iota(jnp.int32, (c, 2 * c), 1)
    col = lane % c
    left = lane < c
    tril = row >= col
    strict = row > col
    eye2 = (row == col).astype(F32)
    br = lax.broadcasted_iota(jnp.int32, (2 * c, 2 * c), 0) // c
    bc = lax.broadcasted_iota(jnp.int32, (2 * c, 2 * c), 1) // c
    same_block = br == bc

    def blockdiag(p):
        return jnp.where(same_block, jnp.concatenate([p, p], axis=0), 0.0)

    def split(v):
        head = v.astype(BF16)
        return head, (v - head.astype(F32)).astype(BF16)

    nw = nw_ref[...]
    zeros_b = jnp.zeros((c, hd), BF16)
    n_pairs = GDN_GROUP // 2
    items = [(cc, jq) for cc in range(n_chunks) for jq in range(n_pairs)]

    ts = n_chunks * c
    raw = gt_ref[0]
    rr = lax.broadcasted_iota(jnp.int32, (ts, ts), 0)
    rc = lax.broadcasted_iota(jnp.int32, (ts, ts), 1)
    tri = jnp.where((rr >= rc) & ((rr // c) == (rc // c)), 1.0, 0.0).astype(BF16)
    gates_all = jnp.where(lax.broadcasted_iota(jnp.int32, raw.shape, 1) < GDN_GROUP, _tri_cumsum(tri, raw), raw)

    chunk = []
    for cc in range(n_chunks):
        rows = slice(cc * c, (cc + 1) * c)
        gates = gates_all[rows, :]
        g_last = gates[c - 1:c, :]
        chunk.append(dict(
            rows=rows, gates=gates,
            gates_t=jnp.concatenate([gates, gates], axis=0).T,
            exp_g=jnp.exp(gates),
            exp_rest=jnp.exp(g_last - gates),
            exp_last=jnp.exp(g_last)))

    it = {}
    for key in items:
        cc, jq = key
        ck = chunk[cc]
        gates, gates_t, rows = ck["gates"], ck["gates_t"], ck["rows"]
        ha, hb = 2 * jq, 2 * jq + 1
        q = q_ref[0, rows, jq * hd:(jq + 1) * hd]
        k = k_ref[0, rows, jq * hd:(jq + 1) * hd]
        kq = lax.dot_general(jnp.concatenate([k, q], axis=0), jnp.concatenate([k, k], axis=0),
                             (((1,), (1,)), ((), ())), preferred_element_type=F32)
        g_col = jnp.where(left, gates[:, ha:ha + 1], gates[:, hb:hb + 1])
        g_row = jnp.where(left[0:1], gates_t[ha:ha + 1, :], gates_t[hb:hb + 1, :])
        beta_a = gates[:, GDN_GROUP + ha:GDN_GROUP + ha + 1]
        beta_b = gates[:, GDN_GROUP + hb:GDN_GROUP + hb + 1]
        decay = jnp.where(tril, jnp.exp(g_col - g_row), 0.0)
        a_mat = jnp.where(strict, jnp.where(left, beta_a, beta_b) * kq[:c] * decay, 0.0)
        it[key] = dict(q=q, k=k, attn=(kq[c:] * decay).astype(BF16), a=a_mat, p=-a_mat, x=eye2 - a_mat,
                       heads=((ha, beta_a), (hb, beta_b)))

    for _ in range(5):
        for key in items:
            d = it[key]
            d["p"] = _bdot(d["p"].astype(BF16), blockdiag(d["p"]).astype(BF16))
        for key in items:
            d = it[key]
            d["x"] = d["x"] + _bdot(d["x"].astype(BF16), blockdiag(d["p"]).astype(BF16))

    for key in items:
        d = it[key]
        a_hi, a_lo = split(eye2 + d["a"])
        x_hi, x_lo = split(blockdiag(d["x"]))
        prod = _bdot(jnp.concatenate([a_hi, a_lo, a_hi], axis=1), jnp.concatenate([x_hi, x_hi, x_lo], axis=0))
        d["r"] = eye2 - prod
    for key in items:
        d = it[key]
        d["x"] = d["x"] + _bdot(d["x"].astype(BF16), blockdiag(d["r"]).astype(BF16))

    for key in items:
        d = it[key]
        ck = chunk[key[0]]
        kf = d["k"].astype(F32)
        vbs, kbs = [], []
        for h, beta in d["heads"]:
            vf = v_ref[0, ck["rows"], h * hd:(h + 1) * hd].astype(F32)
            vbs.append((vf * beta).astype(BF16))
            kbs.append((kf * (beta * ck["exp_g"][:, h:h + 1])).astype(BF16))
        rhs = jnp.concatenate([
            jnp.concatenate([vbs[0], zeros_b, kbs[0], zeros_b], axis=1),
            jnp.concatenate([zeros_b, vbs[1], zeros_b, kbs[1]], axis=1)], axis=0)
        d["uw"] = _bdot(d["x"].astype(BF16), rhs)
        d["kg"] = jnp.concatenate([(kf * ck["exp_rest"][:, h:h + 1]).astype(BF16) for h, _ in d["heads"]], axis=0)
        qf = d["q"].astype(F32)
        d["qg"] = [(qf * ck["exp_g"][:, h:h + 1]).astype(BF16) for h, _ in d["heads"]]

    states = [state_ref[h] for h in range(GDN_GROUP)]
    for cc in range(n_chunks):
        ck = chunk[cc]
        wqs = {}
        for jq in range(n_pairs):
            d = it[(cc, jq)]
            for r, (h, _) in enumerate(d["heads"]):
                w_r = d["uw"][:, (2 + r) * hd:(3 + r) * hd].astype(BF16)
                wqs[h] = _bdot(jnp.concatenate([w_r, d["qg"][r]], axis=0), states[h].astype(BF16))
        outs = {}
        for jq in range(n_pairs):
            d = it[(cc, jq)]
            (ha, _), (hb, _) = d["heads"]
            vna = (d["uw"][:, 0:hd] - wqs[ha][:c]).astype(BF16)
            vnb = (d["uw"][:, hd:2 * hd] - wqs[hb][:c]).astype(BF16)
            vn = jnp.concatenate([jnp.concatenate([vna, zeros_b], axis=1),
                                  jnp.concatenate([zeros_b, vnb], axis=1)], axis=0)
            o_intra = _bdot(d["attn"], vn)
            ds = lax.dot_general(d["kg"], vn, (((0,), (0,)), ((), ())), preferred_element_type=F32)
            for r, h in enumerate((ha, hb)):
                states[h] = states[h] * ck["exp_last"][0:1, h:h + 1] + ds[:, r * hd:(r + 1) * hd]
                outs[h] = wqs[h][c:] + o_intra[:, r * hd:(r + 1) * hd]
        for h in range(GDN_GROUP):
            o = outs[h]
            on = o * lax.rsqrt(jnp.mean(o * o, axis=-1, keepdims=True) + EPS) * nw
            zf = z_ref[0, ck["rows"], h * hd:(h + 1) * hd].astype(F32)
            o_ref[0, ck["rows"], h * hd:(h + 1) * hd] = (on * _silu(zf)).astype(BF16)

    for h in range(GDN_GROUP):
        state_ref[h] = states[h]


def _gdn(qkvz, gates, norm_w, *, batch, seq, qk_dim, v_dim, n_chunks=4):
    total = qkvz.shape[1]
    x3 = qkvz.reshape(batch, seq, total)
    g3 = gates.reshape(batch, seq, gates.shape[1])
    hd = HEAD_DIM
    qw = GDN_GROUP // 2 * hd
    vw = GDN_GROUP * hd
    n_groups = v_dim // vw
    k_off = qk_dim // qw
    v_off = 2 * qk_dim // vw
    z_off = (2 * qk_dim + v_dim) // vw
    ts = n_chunks * CHUNK
    return pl.pallas_call(
        functools.partial(_gdn_kernel, n_chunks=n_chunks),
        grid=(batch, n_groups, seq // ts),
        in_specs=[
            pl.BlockSpec((1, ts, qw), lambda b, g, c: (b, c, g)),
            pl.BlockSpec((1, ts, qw), lambda b, g, c: (b, c, k_off + g)),
            pl.BlockSpec((1, ts, vw), lambda b, g, c: (b, c, v_off + g)),
            pl.BlockSpec((1, ts, vw), lambda b, g, c: (b, c, z_off + g)),
            pl.BlockSpec((1, ts, LANES), lambda b, g, c: (b, c, g)),
            pl.BlockSpec((1, hd), lambda b, g, c: (0, 0)),
        ],
        out_specs=pl.BlockSpec((1, ts, vw), lambda b, g, c: (b, c, g)),
        out_shape=jax.ShapeDtypeStruct((batch, seq, v_dim), BF16),
        scratch_shapes=[pltpu.VMEM((GDN_GROUP, hd, hd), F32)],
        compiler_params=_params(("parallel", "parallel", "arbitrary")),
        name="gdn",
    )(x3, x3, x3, x3, g3, norm_w)


def _grouped_gate_layout(w_b, w_a, a_log, dt_bias):
    d, hv = w_a.shape
    ng = hv // GDN_GROUP
    pad = LANES - 2 * GDN_GROUP
    wa = w_a.reshape(d, ng, GDN_GROUP)
    wb = w_b.reshape(d, ng, GDN_GROUP)
    w = jnp.concatenate([wa, wb, jnp.zeros((d, ng, pad), w_a.dtype)], axis=-1).reshape(d, ng * LANES)
    zg = jnp.zeros((ng, LANES - GDN_GROUP), F32)
    alog = jnp.concatenate([a_log.reshape(ng, GDN_GROUP).astype(F32), zg], axis=-1).reshape(1, ng * LANES)
    dtb = jnp.concatenate([dt_bias.reshape(ng, GDN_GROUP).astype(F32), zg], axis=-1).reshape(1, ng * LANES)
    return w.astype(BF16), alog, dtb


def kernel(x, c, ada_w, ada_b, norm_w, ffn_w_gu, ffn_w_down, ab_w_in, ab_b_f, pool_w, pool_scale, ab_w_out,
           gdn_w_in, gdn_conv_w, gdn_a_log, gdn_dt_bias, gdn_norm_w, gdn_w_out, final_norm_w):
    batch, seq, d = x.shape
    depth = ada_w.shape[0]
    t = batch * seq

    mod_all = _adaln(c, ada_w, ada_b).reshape(depth, batch, N_MOD, d)
    w_gu = ffn_w_gu[0, 0].astype(BF16)
    w_down = ffn_w_down[0, 0].astype(BF16)

    def ffn(xf, mod, w_gu, w_down, layer, half):
        last = layer == depth - 1 and half == 1
        nxt = None if last else (ffn_w_gu, ffn_w_down, layer + half, 1 - half)
        out = _ffn(xf, mod, nw4, w_gu, w_down, fw, layer=layer, half=half, seq=seq, final=last, next_f32=nxt)
        return (out, None, None) if last else out
    fw = final_norm_w.reshape(1, d)
    xf = x.reshape(t, d)

    nw4 = norm_w.reshape(depth, 3, 1, d)
    for layer in range(depth):
        mod = mod_all[layer]
        e = layer // 2
        nws = nw4[layer]
        xf, w_gu, w_down = ffn(xf, mod, w_gu, w_down, layer, 0)

        if layer % 2 == 0:
            pool_width = pool_scale.shape[1]
            fox_heads = ab_b_f.shape[1]
            fox_width = fox_heads * HEAD_DIM
            n_main = pool_width + 3 * fox_width
            w_in = ab_w_in[e]
            w_f = jnp.pad(w_in[:, n_main:], ((0, 0), (0, LANES - fox_heads))).astype(BF16)
            b_f = jnp.pad(ab_b_f[e], (0, LANES - fox_heads)).reshape(1, LANES)
            pool_out, qkv, cum = _abin(xf, mod, nws[1], w_in.astype(BF16), w_f, b_f, pool_w[e].astype(BF16),
                                       pool_scale[e].reshape(1, pool_width), k0=3, seq=seq)
            attn = _fox(qkv, cum, batch=batch, seq=seq, heads=fox_heads)
            w_out = ab_w_out[e].astype(BF16)
            xf = _proj_res(xf, mod, [pool_out, attn.reshape(t, fox_width)], w_out, gate_row=5, seq=seq)
        else:
            v_heads = gdn_a_log.shape[1]
            v_dim = v_heads * HEAD_DIM
            conv_ch = gdn_conv_w.shape[2]
            qk_dim = (conv_ch - v_dim) // 2
            n_main = conv_ch + v_dim
            w_in = gdn_w_in[e]
            w_gate, alog_l, dtb_l = _grouped_gate_layout(
                w_in[:, n_main:n_main + v_heads], w_in[:, n_main + v_heads:], gdn_a_log[e], gdn_dt_bias[e])
            qkvz, gates = _gdnin(xf, mod, nws[1], w_in.astype(BF16), w_gate, alog_l, dtb_l,
                                 gdn_conv_w[e], k0=3, seq=seq, qk_dim=qk_dim, conv_ch=conv_ch, n_main=n_main)
            o = _gdn(qkvz, gates, gdn_norm_w[e].reshape(1, HEAD_DIM), batch=batch, seq=seq,
                     qk_dim=qk_dim, v_dim=v_dim)
            xf = _proj_res(xf, mod, [o.reshape(t, v_dim)], gdn_w_out[e].astype(BF16), gate_row=5, seq=seq)

        xf, w_gu, w_down = ffn(xf, mod, w_gu, w_down, layer, 1)
    return xf.reshape(batch, seq, d)
```

```python
import functools

import jax
import jax.numpy as jnp
from jax import lax
from jax.experimental import pallas as pl
from jax.experimental.pallas import tpu as pltpu

F32 = jnp.float32
BF16 = jnp.bfloat16
EPS = 1e-6

LANES = 128
N_MOD = 9
POOL_WINDOWS = (2, 4, 8, 16)
POOL_HALO = 16
HEAD_DIM = 128
CONV_K = 4
CONV_HALO = 8
CHUNK = 64
GDN_GROUP = 16
NEG_BIG = -1e30
LOG2E = 1.4426950408889634
VMEM_LIMIT = 56 * 1024 * 1024
VMEM_LIMIT_BIG = 61 * 1024 * 1024 + 256 * 1024


def _silu(x):
    return x * jax.nn.sigmoid(x)


def _softplus(x):
    return jnp.maximum(x, 0.0) + jnp.log1p(jnp.exp(-jnp.abs(x)))


def _bdot(a, b):
    return jnp.dot(a, b, preferred_element_type=F32)


def _tri_cumsum(tri, x):
    x1 = x.astype(BF16)
    r1 = x - x1.astype(F32)
    x2 = r1.astype(BF16)
    x3 = (r1 - x2.astype(F32)).astype(BF16)
    s = _bdot(tri, jnp.concatenate([x1, x2, x3], axis=1))
    return s[:, 0:LANES] + s[:, LANES:2 * LANES] + s[:, 2 * LANES:3 * LANES]


def _norm_mod(x, nw, shift, scale):
    y = x * lax.rsqrt(jnp.mean(x * x, axis=-1, keepdims=True) + EPS) * nw
    return y * (1.0 + scale) + shift


def _params(sem, vmem=VMEM_LIMIT):
    return pltpu.CompilerParams(dimension_semantics=sem, vmem_limit_bytes=vmem)


def _adaln_kernel(c_ref, w_top_ref, w_bot_ref, b_ref, o_ref):
    ca = _silu(c_ref[...]).astype(BF16)
    half = w_top_ref.shape[1]
    o_ref[0] = (_bdot(ca[:, :half], w_top_ref[0].astype(BF16)) + _bdot(ca[:, half:], w_bot_ref[0].astype(BF16))
                + b_ref[0])


def _adaln(c, ada_w, ada_b):
    depth, d, n = ada_w.shape
    b = c.shape[0]
    tn = 2048
    return pl.pallas_call(
        _adaln_kernel,
        grid=(depth, n // tn),
        in_specs=[
            pl.BlockSpec((b, d), lambda l, j: (0, 0)),
            pl.BlockSpec((1, d // 2, tn), lambda l, j: (l, 0, j)),
            pl.BlockSpec((1, d // 2, tn), lambda l, j: (l, 1, j)),
            pl.BlockSpec((1, 1, tn), lambda l, j: (l, 0, j)),
        ],
        out_specs=pl.BlockSpec((1, b, tn), lambda l, j: (l, 0, j)),
        out_shape=jax.ShapeDtypeStruct((depth, b, n), F32),
        compiler_params=_params(("parallel", "parallel")),
        name="adaln",
    )(c, ada_w, ada_w, ada_b.reshape(depth, 1, n))


def _ffn_kernel(*refs, k0, n_ff, final, cast_next):
    x_ref, mod_ref, nw_ref, wg_ref, wu_ref, wd_ref, fw_ref = refs[:7]
    if cast_next:
        gu_src, down_src, o_ref, gu_dst, down_dst, h_ref = refs[7:]
        gu_dst[...] = gu_src[...].astype(BF16)
        down_dst[...] = down_src[...].astype(BF16)
    else:
        o_ref, h_ref = refs[7:]
    j = pl.program_id(1)

    def accumulate(base_ref):
        half_gate = 0.5 * mod_ref[0, k0 + 2:k0 + 3, :]
        rows = h_ref.shape[0] // 2
        for r in range(2):
            sl = slice(r * rows, (r + 1) * rows)
            h = h_ref[sl, :]
            g = _bdot(h, wg_ref[...])
            u = _bdot(h, wu_ref[...])
            o_ref[sl, :] = base_ref[sl, :] + half_gate * _bdot((_silu(g) * u).astype(BF16), wd_ref[...])

    @pl.when(j == 0)
    def _():
        h = _norm_mod(x_ref[...], nw_ref[...], mod_ref[0, k0:k0 + 1, :], mod_ref[0, k0 + 1:k0 + 2, :])
        h_ref[...] = h.astype(BF16)
        accumulate(x_ref)

    @pl.when(j > 0)
    def _():
        accumulate(o_ref)

    if final:
        @pl.when(j == n_ff - 1)
        def _():
            xn = o_ref[...]
            o_ref[...] = xn * lax.rsqrt(jnp.mean(xn * xn, axis=-1, keepdims=True) + EPS) * fw_ref[...]


def _ffn(x, mod, nw, w_gu, w_down, fw, *, layer, half, seq, final, next_f32=None, tm=1024, tf=512):
    t, d = x.shape
    dff = w_down.shape[0]
    n_i, n_ff = t // tm, dff // tf
    tpb = seq // tm
    k0 = 6 * half
    cast_next = next_f32 is not None
    kern = functools.partial(_ffn_kernel, k0=k0, n_ff=n_ff, final=final, cast_next=cast_next)
    in_specs = [
        pl.BlockSpec((tm, d), lambda i, j: (i, 0)),
        pl.BlockSpec((1, N_MOD, d), lambda i, j: (i // tpb, 0, 0)),
        pl.BlockSpec((None, None, 1, d), lambda i, j: (layer, 2 * half, 0, 0)),
        pl.BlockSpec((d, tf), lambda i, j: (0, j)),
        pl.BlockSpec((d, tf), lambda i, j: (0, j + n_ff)),
        pl.BlockSpec((tf, d), lambda i, j: (j, 0)),
        pl.BlockSpec((1, d), lambda i, j: (0, 0)),
    ]
    out_specs = [pl.BlockSpec((tm, d), lambda i, j: (i, 0))]
    out_shape = [jax.ShapeDtypeStruct((t, d), F32)]
    args = [x, mod, nw, w_gu, w_gu, w_down, fw]
    if cast_next:
        gu32, down32, nl, nh = next_f32
        gu_blk = (d // n_i, 2 * dff // n_ff)
        down_blk = (dff // n_ff, d // n_i)
        in_specs += [pl.BlockSpec((None, None) + gu_blk, lambda i, j: (nl, nh, i, j)),
                     pl.BlockSpec((None, None) + down_blk, lambda i, j: (nl, nh, j, i))]
        out_specs += [pl.BlockSpec(gu_blk, lambda i, j: (i, j)), pl.BlockSpec(down_blk, lambda i, j: (j, i))]
        out_shape += [jax.ShapeDtypeStruct((d, 2 * dff), BF16), jax.ShapeDtypeStruct((dff, d), BF16)]
        args += [gu32, down32]
    outs = pl.pallas_call(
        kern,
        grid=(n_i, n_ff),
        in_specs=in_specs,
        out_specs=out_specs,
        out_shape=out_shape,
        scratch_shapes=[pltpu.VMEM((tm, d), BF16)],
        compiler_params=_params(("parallel", "arbitrary"), vmem=VMEM_LIMIT_BIG),
        name="ffn",
    )(*args)
    return outs if cast_next else outs[0]


def _abin_kernel(x_ref, mod_ref, nw_ref, w_ref, wf_ref, bf_ref, pw_ref, ps_ref,
                 pool_ref, qkv_ref, cum_ref, h_ref, ubuf, fcarry, *, k0, tm, tpb, gw, tn):
    i = pl.program_id(0)
    first = (i % tpb) == 0

    h = _norm_mod(x_ref[...], nw_ref[...], mod_ref[0, k0:k0 + 1, :], mod_ref[0, k0 + 1:k0 + 2, :])
    h_ref[...] = h.astype(BF16)
    hb = h_ref[...]

    for part in range(3):
        acc = _bdot(hb, w_ref[:, (part + 1) * tn:(part + 2) * tn])
        if part == 0:
            acc = acc * (LOG2E * HEAD_DIM ** -0.5)
        qkv_ref[:, part * tn:(part + 1) * tn] = acc.astype(BF16)

    f = _bdot(hb, wf_ref[...]) + bf_ref[...]
    log_f = jnp.minimum(f, 0.0) - jnp.log1p(jnp.exp(-jnp.abs(f)))
    r = lax.broadcasted_iota(jnp.int32, (tm, tm), 0)
    c = lax.broadcasted_iota(jnp.int32, (tm, tm), 1)
    tri = jnp.where(r >= c, 1.0, 0.0).astype(BF16)

    @pl.when(first)
    def _():
        fcarry[...] = jnp.zeros_like(fcarry)
        ubuf[0:POOL_HALO, :] = jnp.zeros((POOL_HALO, ubuf.shape[1]), F32)

    cum = _tri_cumsum(tri, log_f) + fcarry[...]
    cum_ref[...] = cum
    fcarry[...] = cum[tm - 1:tm, :]

    u = _bdot(hb, w_ref[:, 0:tn])
    ubuf[POOL_HALO:POOL_HALO + tm, :] = u
    pos = (i % tpb) * tm + lax.broadcasted_iota(jnp.int32, (tm, 1), 0)
    for g, win in enumerate(POOL_WINDOWS):
        sl = slice(g * gw, (g + 1) * gw)
        acc = u[:, sl]
        for s in range(1, win):
            acc = acc + ubuf[POOL_HALO - s:POOL_HALO - s + tm, sl]
        cnt = jnp.minimum(pos + 1, win).astype(F32)
        dlt = acc / cnt - u[:, sl]
        y = _bdot(dlt.astype(BF16), pw_ref[g]) * ps_ref[:, sl]
        pool_ref[:, sl] = y.astype(BF16)
    ubuf[0:POOL_HALO, :] = ubuf[tm:tm + POOL_HALO, :]


def _abin(x, mod, nw, w_main, w_f, b_f, pool_w, pool_scale, *, k0, seq, tm=512):
    t, d = x.shape
    tn = pool_scale.shape[1]
    tpb = seq // tm
    groups = pool_w.shape[0]
    gw = tn // groups
    kern = functools.partial(_abin_kernel, k0=k0, tm=tm, tpb=tpb, gw=gw, tn=tn)
    return pl.pallas_call(
        kern,
        grid=(t // tm,),
        in_specs=[
            pl.BlockSpec((tm, d), lambda i: (i, 0)),
            pl.BlockSpec((1, N_MOD, d), lambda i: (i // tpb, 0, 0)),
            pl.BlockSpec((1, d), lambda i: (0, 0)),
            pl.BlockSpec(w_main.shape, lambda i: (0, 0), pipeline_mode=pl.Buffered(1)),
            pl.BlockSpec((d, LANES), lambda i: (0, 0)),
            pl.BlockSpec((1, LANES), lambda i: (0, 0)),
            pl.BlockSpec((groups, gw, gw), lambda i: (0, 0, 0)),
            pl.BlockSpec((1, tn), lambda i: (0, 0)),
        ],
        out_specs=[
            pl.BlockSpec((tm, tn), lambda i: (i, 0)),
            pl.BlockSpec((tm, 3 * tn), lambda i: (i, 0)),
            pl.BlockSpec((tm, LANES), lambda i: (i, 0)),
        ],
        out_shape=[
            jax.ShapeDtypeStruct((t, tn), BF16),
            jax.ShapeDtypeStruct((t, 3 * tn), BF16),
            jax.ShapeDtypeStruct((t, LANES), F32),
        ],
        scratch_shapes=[
            pltpu.VMEM((tm, d), BF16),
            pltpu.VMEM((POOL_HALO + tm, tn), F32),
            pltpu.VMEM((1, LANES), F32),
        ],
        compiler_params=_params(("arbitrary",)),
        name="abin",
    )(x, mod, nw, w_main, w_f, b_f, pool_w, pool_scale)


def _fox_kernel(q_ref, k_ref, v_ref, cum_ref, o_ref, qa_ref, ka_ref, va_ref, *, tq, seq):
    h = pl.program_id(1)
    hd = HEAD_DIM
    cum = cum_ref[0]
    lane = lax.broadcasted_iota(jnp.int32, cum.shape, 1)
    f = jnp.sum(jnp.where(lane == h, cum, 0.0), axis=-1, keepdims=True) * LOG2E
    f_hi = f.astype(BF16).astype(F32)
    r1 = f - f_hi
    f_mid = r1.astype(BF16).astype(F32)
    f_lo = r1 - f_mid
    zero = jnp.zeros_like(cum)
    one_or_zero = jnp.where(lane < 6, 1.0, 0.0)
    fq = jnp.where(lane == 0, f_hi, jnp.where(lane == 1, f_mid, jnp.where(lane == 2, f_lo, one_or_zero)))
    fk = jnp.where(lane < 3, 1.0,
                   jnp.where(lane == 3, -f_hi, jnp.where(lane == 4, -f_mid, jnp.where(lane == 5, -f_lo, zero))))
    qa_ref[:, 0:hd] = q_ref[0]
    qa_ref[:, hd:2 * hd] = fq.astype(BF16)
    ka_ref[:, 0:hd] = k_ref[0]
    ka_ref[:, hd:2 * hd] = fk.astype(BF16)
    va_ref[:, 0:hd] = v_ref[0]
    va_ref[:, hd:2 * hd] = jnp.ones((seq, hd), BF16)

    r = lax.broadcasted_iota(jnp.int32, (tq, tq), 0)
    c = lax.broadcasted_iota(jnp.int32, (tq, tq), 1)
    causal = c <= r

    nq = seq // tq
    m = [jnp.full((tq, 1), NEG_BIG, F32) for _ in range(nq)]
    acc = [jnp.zeros((tq, 2 * hd), F32) for _ in range(nq)]
    for j in range(nq):
        ka = ka_ref[j * tq:(j + 1) * tq, :]
        va = va_ref[j * tq:(j + 1) * tq, :]
        for qi in range(j, nq):
            s = lax.dot_general(qa_ref[qi * tq:(qi + 1) * tq, :], ka, (((1,), (1,)), ((), ())),
                                preferred_element_type=F32)
            if qi == j:
                s = jnp.where(causal, s, NEG_BIG)
            m_new = jnp.maximum(m[qi], jnp.max(s, axis=-1, keepdims=True))
            p = jnp.exp2(s - m_new)
            acc[qi] = jnp.exp2(m[qi] - m_new) * acc[qi] + _bdot(p.astype(BF16), va)
            m[qi] = m_new
    for qi in range(nq):
        o_ref[0, qi * tq:(qi + 1) * tq, :] = (acc[qi][:, 0:hd] / acc[qi][:, hd:2 * hd]).astype(BF16)


def _fox(qkv, cum, *, batch, seq, heads, tq=256):
    qkv3 = qkv.reshape(batch, seq, 3 * heads * HEAD_DIM)
    cum3 = cum.reshape(batch, seq, LANES)
    kern = functools.partial(_fox_kernel, tq=tq, seq=seq)
    return pl.pallas_call(
        kern,
        grid=(batch, heads),
        in_specs=[
            pl.BlockSpec((1, seq, HEAD_DIM), lambda b, h: (b, 0, h)),
            pl.BlockSpec((1, seq, HEAD_DIM), lambda b, h: (b, 0, heads + h)),
            pl.BlockSpec((1, seq, HEAD_DIM), lambda b, h: (b, 0, 2 * heads + h)),
            pl.BlockSpec((1, seq, LANES), lambda b, h: (b, 0, 0)),
        ],
        out_specs=pl.BlockSpec((1, seq, HEAD_DIM), lambda b, h: (b, 0, h)),
        out_shape=jax.ShapeDtypeStruct((batch, seq, heads * HEAD_DIM), BF16),
        scratch_shapes=[pltpu.VMEM((seq, 2 * HEAD_DIM), BF16)] * 3,
        compiler_params=_params(("parallel", "parallel")),
        name="fox",
    )(qkv3, qkv3, qkv3, cum3)


def _proj_res_kernel(*refs, n_in, gate_row):
    x_ref, mod_ref = refs[0], refs[1]
    a_refs = refs[2:2 + n_in]
    w_ref, o_ref = refs[2 + n_in], refs[3 + n_in]
    y = None
    row = 0
    for a_ref in a_refs:
        ka = a_ref.shape[1]
        part = _bdot(a_ref[...], w_ref[row:row + ka, :])
        y = part if y is None else y + part
        row += ka
    o_ref[...] = x_ref[...] + mod_ref[0, gate_row:gate_row + 1, :] * y


def _proj_res(x, mod, acts, weight, *, gate_row, seq, tm=512):
    t, d = x.shape
    tpb = seq // tm
    n_in = len(acts)
    kern = functools.partial(_proj_res_kernel, n_in=n_in, gate_row=gate_row)
    in_specs = [
        pl.BlockSpec((tm, d), lambda i: (i, 0)),
        pl.BlockSpec((1, N_MOD, d), lambda i: (i // tpb, 0, 0)),
    ]
    in_specs += [pl.BlockSpec((tm, a.shape[1]), lambda i: (i, 0)) for a in acts]
    in_specs += [pl.BlockSpec(weight.shape, lambda i: (0, 0), pipeline_mode=pl.Buffered(1))]
    return pl.pallas_call(
        kern,
        grid=(t // tm,),
        in_specs=in_specs,
        out_specs=pl.BlockSpec((tm, d), lambda i: (i, 0)),
        out_shape=jax.ShapeDtypeStruct((t, d), F32),
        compiler_params=_params(("parallel",)),
        name="proj_res",
    )(x, mod, *acts, weight)


def _gdnin_kernel(x_ref, mod_ref, nw_ref, w_ref, wg_ref, alog_ref, dtb_ref, cw_ref,
                  o_ref, gate_ref, h_ref, pbuf, carry_ref, *, k0, tm, tpb, n_conv, n_qk, n_q):
    i = pl.program_id(0)
    n = pl.program_id(1)
    first = (i % tpb) == 0

    @pl.when(n == 0)
    def _():
        h = _norm_mod(x_ref[...], nw_ref[...], mod_ref[0, k0:k0 + 1, :], mod_ref[0, k0 + 1:k0 + 2, :])
        h_ref[...] = h.astype(BF16)
        ba = _bdot(h_ref[...], wg_ref[...])
        lane = lax.broadcasted_iota(jnp.int32, ba.shape, 1) % LANES
        g = -jnp.exp(alog_ref[...]) * _softplus(ba + dtb_ref[...])
        gate_ref[...] = jnp.where(lane < GDN_GROUP, g, jax.nn.sigmoid(ba))

        @pl.when(i == 0)
        def _():
            carry_ref[...] = jnp.zeros_like(carry_ref)

    def conv_silu():
        acc = _bdot(h_ref[...], w_ref[...])
        pbuf[0:CONV_HALO, :] = jnp.where(first, 0.0, carry_ref[n])
        pbuf[CONV_HALO:CONV_HALO + tm, :] = acc
        carry_ref[n] = acc[tm - CONV_HALO:tm, :]
        y = cw_ref[CONV_K - 1:CONV_K, :] * acc
        for kk in range(CONV_K - 1):
            lo = CONV_HALO - (CONV_K - 1) + kk
            y = y + cw_ref[kk:kk + 1, :] * pbuf[lo:lo + tm, :]
        return _silu(y)

    @pl.when(n < n_qk)
    def _():
        y = conv_silu()
        qscale = jnp.where(n < n_q, HEAD_DIM ** -0.5, 1.0).astype(F32)
        for hh in range(y.shape[1] // HEAD_DIM):
            seg = y[:, hh * HEAD_DIM:(hh + 1) * HEAD_DIM]
            inv = lax.rsqrt(jnp.sum(seg * seg, axis=-1, keepdims=True) + EPS) * qscale
            o_ref[:, hh * HEAD_DIM:(hh + 1) * HEAD_DIM] = (seg * inv).astype(BF16)

    @pl.when((n >= n_qk) & (n < n_conv))
    def _():
        o_ref[...] = conv_silu().astype(BF16)

    @pl.when(n >= n_conv)
    def _():
        o_ref[...] = _bdot(h_ref[...], w_ref[...]).astype(BF16)


def _gdnin(x, mod, nw, w_main, w_gate, alog_l, dtb_l, conv_w, *, k0, seq, qk_dim, conv_ch, n_main,
           tm=512, tn=2048):
    t, d = x.shape
    n_total = n_main
    tpb = seq // tm
    n_conv = conv_ch // tn
    n_qk = 2 * qk_dim // tn
    n_q = qk_dim // tn
    gl = w_gate.shape[1]
    kern = functools.partial(_gdnin_kernel, k0=k0, tm=tm, tpb=tpb, n_conv=n_conv, n_qk=n_qk, n_q=n_q)
    return pl.pallas_call(
        kern,
        grid=(t // tm, n_total // tn),
        in_specs=[
            pl.BlockSpec((tm, d), lambda i, n: (i, 0)),
            pl.BlockSpec((1, N_MOD, d), lambda i, n: (i // tpb, 0, 0)),
            pl.BlockSpec((1, d), lambda i, n: (0, 0)),
            pl.BlockSpec((d, tn), lambda i, n: (0, n)),
            pl.BlockSpec((d, gl), lambda i, n: (0, 0)),
            pl.BlockSpec((1, gl), lambda i, n: (0, 0)),
            pl.BlockSpec((1, gl), lambda i, n: (0, 0)),
            pl.BlockSpec((CONV_K, tn), lambda i, n: (0, jnp.minimum(n, n_conv - 1))),
        ],
        out_specs=[
            pl.BlockSpec((tm, tn), lambda i, n: (i, n)),
            pl.BlockSpec((tm, gl), lambda i, n: (i, 0)),
        ],
        out_shape=[
            jax.ShapeDtypeStruct((t, n_total), BF16),
            jax.ShapeDtypeStruct((t, gl), F32),
        ],
        scratch_shapes=[
            pltpu.VMEM((tm, d), BF16),
            pltpu.VMEM((CONV_HALO + tm, tn), F32),
            pltpu.VMEM((n_conv, CONV_HALO, tn), F32),
        ],
        compiler_params=_params(("arbitrary", "arbitrary")),
        name="gdnin",
    )(x, mod, nw, w_main, w_gate, alog_l, dtb_l, conv_w)


def _gdn_kernel(q_ref, k_ref, v_ref, z_ref, gt_ref, nw_ref, o_ref, state_ref, *, n_chunks):
    @pl.when(pl.program_id(2) == 0)
    def _():
        state_ref[...] = jnp.zeros_like(state_ref)

    c = CHUNK
    hd = HEAD_DIM
    row = lax.broadcasted_iota(jnp.int32, (c, 2 * c), 0)
    lane = lax.broadcasted_iota(jnp.int32, (c, 2 * c), 1)
    col = lane % c
    left = lane < c
    tril = row >= col
    strict = row > col
    eye2 = (row == col).astype(F32)
    br = lax.broadcasted_iota(jnp.int32, (2 * c, 2 * c), 0) // c
    bc = lax.broadcasted_iota(jnp.int32, (2 * c, 2 * c), 1) // c
    same_block = br == bc

    def blockdiag(p):
        return jnp.where(same_block, jnp.concatenate([p, p], axis=0), 0.0)

    def split(v):
        head = v.astype(BF16)
        return head, (v - head.astype(F32)).astype(BF16)

    nw = nw_ref[...]
    zeros_b = jnp.zeros((c, hd), BF16)
    n_pairs = GDN_GROUP // 2
    items = [(cc, jq) for cc in range(n_chunks) for jq in range(n_pairs)]

    ts = n_chunks * c
    raw = gt_ref[0]
    rr = lax.broadcasted_iota(jnp.int32, (ts, ts), 0)
    rc = lax.broadcasted_iota(jnp.int32, (ts, ts), 1)
    tri = jnp.where((rr >= rc) & ((rr // c) == (rc // c)), 1.0, 0.0).astype(BF16)
    gates_all = jnp.where(lax.broadcasted_iota(jnp.int32, raw.shape, 1) < GDN_GROUP, _tri_cumsum(tri, raw), raw)

    chunk = []
    for cc in range(n_chunks):
        rows = slice(cc * c, (cc + 1) * c)
        gates = gates_all[rows, :]
        g_last = gates[c - 1:c, :]
        chunk.append(dict(
            rows=rows, gates=gates,
            gates_t=jnp.concatenate([gates, gates], axis=0).T,
            exp_g=jnp.exp(gates),
            exp_rest=jnp.exp(g_last - gates),
            exp_last=jnp.exp(g_last)))

    it = {}
    for key in items:
        cc, jq = key
        ck = chunk[cc]
        gates, gates_t, rows = ck["gates"], ck["gates_t"], ck["rows"]
        ha, hb = 2 * jq, 2 * jq + 1
        q = q_ref[0, rows, jq * hd:(jq + 1) * hd]
        k = k_ref[0, rows, jq * hd:(jq + 1) * hd]
        kq = lax.dot_general(jnp.concatenate([k, q], axis=0), jnp.concatenate([k, k], axis=0),
                             (((1,), (1,)), ((), ())), preferred_element_type=F32)
        g_col = jnp.where(left, gates[:, ha:ha + 1], gates[:, hb:hb + 1])
        g_row = jnp.where(left[0:1], gates_t[ha:ha + 1, :], gates_t[hb:hb + 1, :])
        beta_a = gates[:, GDN_GROUP + ha:GDN_GROUP + ha + 1]
        beta_b = gates[:, GDN_GROUP + hb:GDN_GROUP + hb + 1]
        decay = jnp.where(tril, jnp.exp(g_col - g_row), 0.0)
        a_mat = jnp.where(strict, jnp.where(left, beta_a, beta_b) * kq[:c] * decay, 0.0)
        it[key] = dict(q=q, k=k, attn=(kq[c:] * decay).astype(BF16), a=a_mat, p=-a_mat, x=eye2 - a_mat,
                       heads=((ha, beta_a), (hb, beta_b)))

    for _ in range(5):
        for key in items:
            d = it[key]
            d["p"] = _bdot(d["p"].astype(BF16), blockdiag(d["p"]).astype(BF16))
        for key in items:
            d = it[key]
            d["x"] = d["x"] + _bdot(d["x"].astype(BF16), blockdiag(d["p"]).astype(BF16))

    for key in items:
        d = it[key]
        a_hi, a_lo = split(eye2 + d["a"])
        x_hi, x_lo = split(blockdiag(d["x"]))
        prod = _bdot(jnp.concatenate([a_hi, a_lo, a_hi], axis=1), jnp.concatenate([x_hi, x_hi, x_lo], axis=0))
        d["r"] = eye2 - prod
    for key in items:
        d = it[key]
        d["x"] = d["x"] + _bdot(d["x"].astype(BF16), blockdiag(d["r"]).astype(BF16))

    for key in items:
        d = it[key]
        ck = chunk[key[0]]
        kf = d["k"].astype(F32)
        vbs, kbs = [], []
        for h, beta in d["heads"]:
            vf = v_ref[0, ck["rows"], h * hd:(h + 1) * hd].astype(F32)
            vbs.append((vf * beta).astype(BF16))
            kbs.append((kf * (beta * ck["exp_g"][:, h:h + 1])).astype(BF16))
        rhs = jnp.concatenate([
            jnp.concatenate([vbs[0], zeros_b, kbs[0], zeros_b], axis=1),
            jnp.concatenate([zeros_b, vbs[1], zeros_b, kbs[1]], axis=1)], axis=0)
        d["uw"] = _bdot(d["x"].astype(BF16), rhs)
        d["kg"] = jnp.concatenate([(kf * ck["exp_rest"][:, h:h + 1]).astype(BF16) for h, _ in d["heads"]], axis=0)
        qf = d["q"].astype(F32)
        d["qg"] = [(qf * ck["exp_g"][:, h:h + 1]).astype(BF16) for h, _ in d["heads"]]

    states = [state_ref[h] for h in range(GDN_GROUP)]
    for cc in range(n_chunks):
        ck = chunk[cc]
        wqs = {}
        for jq in range(n_pairs):
            d = it[(cc, jq)]
            for r, (h, _) in enumerate(d["heads"]):
                w_r = d["uw"][:, (2 + r) * hd:(3 + r) * hd].astype(BF16)
                wqs[h] = _bdot(jnp.concatenate([w_r, d["qg"][r]], axis=0), states[h].astype(BF16))
        outs = {}
        for jq in range(n_pairs):
            d = it[(cc, jq)]
            (ha, _), (hb, _) = d["heads"]
            vna = (d["uw"][:, 0:hd] - wqs[ha][:c]).astype(BF16)
            vnb = (d["uw"][:, hd:2 * hd] - wqs[hb][:c]).astype(BF16)
            vn = jnp.concatenate([jnp.concatenate([vna, zeros_b], axis=1),
                                  jnp.concatenate([zeros_b, vnb], axis=1)], axis=0)
            o_intra = _bdot(d["attn"], vn)
            ds = lax.dot_general(d["kg"], vn, (((0,), (0,)), ((), ())), preferred_element_type=F32)
            for r, h in enumerate((ha, hb)):
                states[h] = states[h] * ck["exp_last"][0:1, h:h + 1] + ds[:, r * hd:(r + 1) * hd]
                outs[h] = wqs[h][c:] + o_intra[:, r * hd:(r + 1) * hd]
        for h in range(GDN_GROUP):
            o = outs[h]
            on = o * lax.rsqrt(jnp.mean(o * o, axis=-1, keepdims=True) + EPS) * nw
            zf = z_ref[0, ck["rows"], h * hd:(h + 1) * hd].astype(F32)
            o_ref[0, ck["rows"], h * hd:(h + 1) * hd] = (on * _silu(zf)).astype(BF16)

    for h in range(GDN_GROUP):
        state_ref[h] = states[h]


def _gdn(qkvz, gates, norm_w, *, batch, seq, qk_dim, v_dim, n_chunks=4):
    total = qkvz.shape[1]
    x3 = qkvz.reshape(batch, seq, total)
    g3 = gates.reshape(batch, seq, gates.shape[1])
    hd = HEAD_DIM
    qw = GDN_GROUP // 2 * hd
    vw = GDN_GROUP * hd
    n_groups = v_dim // vw
    k_off = qk_dim // qw
    v_off = 2 * qk_dim // vw
    z_off = (2 * qk_dim + v_dim) // vw
    ts = n_chunks * CHUNK
    return pl.pallas_call(
        functools.partial(_gdn_kernel, n_chunks=n_chunks),
        grid=(batch, n_groups, seq // ts),
        in_specs=[
            pl.BlockSpec((1, ts, qw), lambda b, g, c: (b, c, g)),
            pl.BlockSpec((1, ts, qw), lambda b, g, c: (b, c, k_off + g)),
            pl.BlockSpec((1, ts, vw), lambda b, g, c: (b, c, v_off + g)),
            pl.BlockSpec((1, ts, vw), lambda b, g, c: (b, c, z_off + g)),
            pl.BlockSpec((1, ts, LANES), lambda b, g, c: (b, c, g)),
            pl.BlockSpec((1, hd), lambda b, g, c: (0, 0)),
        ],
        out_specs=pl.BlockSpec((1, ts, vw), lambda b, g, c: (b, c, g)),
        out_shape=jax.ShapeDtypeStruct((batch, seq, v_dim), BF16),
        scratch_shapes=[pltpu.VMEM((GDN_GROUP, hd, hd), F32)],
        compiler_params=_params(("parallel", "parallel", "arbitrary")),
        name="gdn",
    )(x3, x3, x3, x3, g3, norm_w)


def _grouped_gate_layout(w_b, w_a, a_log, dt_bias):
    d, hv = w_a.shape
    ng = hv // GDN_GROUP
    pad = LANES - 2 * GDN_GROUP
    wa = w_a.reshape(d, ng, GDN_GROUP)
    wb = w_b.reshape(d, ng, GDN_GROUP)
    w = jnp.concatenate([wa, wb, jnp.zeros((d, ng, pad), w_a.dtype)], axis=-1).reshape(d, ng * LANES)
    zg = jnp.zeros((ng, LANES - GDN_GROUP), F32)
    alog = jnp.concatenate([a_log.reshape(ng, GDN_GROUP).astype(F32), zg], axis=-1).reshape(1, ng * LANES)
    dtb = jnp.concatenate([dt_bias.reshape(ng, GDN_GROUP).astype(F32), zg], axis=-1).reshape(1, ng * LANES)
    return w.astype(BF16), alog, dtb


def kernel(x, c, ada_w, ada_b, norm_w, ffn_w_gu, ffn_w_down, ab_w_in, ab_b_f, pool_w, pool_scale, ab_w_out,
           gdn_w_in, gdn_conv_w, gdn_a_log, gdn_dt_bias, gdn_norm_w, gdn_w_out, final_norm_w):
    batch, seq, d = x.shape
    depth = ada_w.shape[0]
    t = batch * seq

    mod_all = _adaln(c, ada_w, ada_b).reshape(depth, batch, N_MOD, d)
    w_gu = ffn_w_gu[0, 0].astype(BF16)
    w_down = ffn_w_down[0, 0].astype(BF16)

    def ffn(xf, mod, w_gu, w_down, layer, half):
        last = layer == depth - 1 and half == 1
        nxt = None if last else (ffn_w_gu, ffn_w_down, layer + half, 1 - half)
        out = _ffn(xf, mod, nw4, w_gu, w_down, fw, layer=layer, half=half, seq=seq, final=last, next_f32=nxt)
        return (out, None, None) if last else out
    fw = final_norm_w.reshape(1, d)
    xf = x.reshape(t, d)

    nw4 = norm_w.reshape(depth, 3, 1, d)
    for layer in range(depth):
        mod = mod_all[layer]
        e = layer // 2
        nws = nw4[layer]
        xf, w_gu, w_down = ffn(xf, mod, w_gu, w_down, layer, 0)

        if layer % 2 == 0:
            pool_width = pool_scale.shape[1]
            fox_heads = ab_b_f.shape[1]
            fox_width = fox_heads * HEAD_DIM
            n_main = pool_width + 3 * fox_width
            w_in = ab_w_in[e]
            w_f = jnp.pad(w_in[:, n_main:], ((0, 0), (0, LANES - fox_heads))).astype(BF16)
            b_f = jnp.pad(ab_b_f[e], (0, LANES - fox_heads)).reshape(1, LANES)
            pool_out, qkv, cum = _abin(xf, mod, nws[1], w_in.astype(BF16), w_f, b_f, pool_w[e].astype(BF16),
                                       pool_scale[e].reshape(1, pool_width), k0=3, seq=seq)
            attn = _fox(qkv, cum, batch=batch, seq=seq, heads=fox_heads)
            w_out = ab_w_out[e].astype(BF16)
            xf = _proj_res(xf, mod, [pool_out, attn.reshape(t, fox_width)], w_out, gate_row=5, seq=seq)
        else:
            v_heads = gdn_a_log.shape[1]
            v_dim = v_heads * HEAD_DIM
            conv_ch = gdn_conv_w.shape[2]
            qk_dim = (conv_ch - v_dim) // 2
            n_main = conv_ch + v_dim
            w_in = gdn_w_in[e]
            w_gate, alog_l, dtb_l = _grouped_gate_layout(
                w_in[:, n_main:n_main + v_heads], w_in[:, n_main + v_heads:], gdn_a_log[e], gdn_dt_bias[e])
            qkvz, gates = _gdnin(xf, mod, nws[1], w_in.astype(BF16), w_gate, alog_l, dtb_l,
                                 gdn_conv_w[e], k0=3, seq=seq, qk_dim=qk_dim, conv_ch=conv_ch, n_main=n_main)
            o = _gdn(qkvz, gates, gdn_norm_w[e].reshape(1, HEAD_DIM), batch=batch, seq=seq,
                     qk_dim=qk_dim, v_dim=v_dim)
            xf = _proj_res(xf, mod, [o.reshape(t, v_dim)], gdn_w_out[e].astype(BF16), gate_row=5, seq=seq)

        xf, w_gu, w_down = ffn(xf, mod, w_gu, w_down, layer, 1)
    return xf.reshape(batch, seq, d)
```

```python
import functools

import jax
import jax.numpy as jnp
from jax import lax
from jax.experimental import pallas as pl
from jax.experimental.pallas import tpu as pltpu

F32 = jnp.float32
BF16 = jnp.bfloat16
EPS = 1e-6

LANES = 128
N_MOD = 9
POOL_WINDOWS = (2, 4, 8, 16)
POOL_HALO = 16
HEAD_DIM = 128
CONV_K = 4
CONV_HALO = 8
CHUNK = 64
GDN_GROUP = 16
NEG_BIG = -1e30
LOG2E = 1.4426950408889634
EXTRA_COL_BLOCKS = 8
VMEM_LIMIT = 56 * 1024 * 1024
VMEM_LIMIT_BIG = 61 * 1024 * 1024 + 256 * 1024


def _silu(x):
    return x * jax.nn.sigmoid(x)


def _softplus(x):
    return jnp.maximum(x, 0.0) + jnp.log1p(jnp.exp(-jnp.abs(x)))


def _bdot(a, b):
    return jnp.dot(a, b, preferred_element_type=F32)


def _tri_cumsum(tri, x):
    x1 = x.astype(BF16)
    r1 = x - x1.astype(F32)
    x2 = r1.astype(BF16)
    x3 = (r1 - x2.astype(F32)).astype(BF16)
    s = _bdot(tri, jnp.concatenate([x1, x2, x3], axis=1))
    return s[:, 0:LANES] + s[:, LANES:2 * LANES] + s[:, 2 * LANES:3 * LANES]


def _norm_mod(x, nw, shift, scale):
    y = x * lax.rsqrt(jnp.mean(x * x, axis=-1, keepdims=True) + EPS) * nw
    return y * (1.0 + scale) + shift


def _params(sem, vmem=VMEM_LIMIT):
    return pltpu.CompilerParams(dimension_semantics=sem, vmem_limit_bytes=vmem)


def _adaln_kernel(c_ref, w_top_ref, w_bot_ref, b_ref, o_ref):
    ca = _silu(c_ref[...]).astype(BF16)
    half = w_top_ref.shape[1]
    o_ref[0] = (_bdot(ca[:, :half], w_top_ref[0].astype(BF16)) + _bdot(ca[:, half:], w_bot_ref[0].astype(BF16))
                + b_ref[0])


def _adaln(c, ada_w, ada_b):
    depth, d, n = ada_w.shape
    b = c.shape[0]
    tn = 2048
    return pl.pallas_call(
        _adaln_kernel,
        grid=(depth, n // tn),
        in_specs=[
            pl.BlockSpec((b, d), lambda l, j: (0, 0)),
            pl.BlockSpec((1, d // 2, tn), lambda l, j: (l, 0, j)),
            pl.BlockSpec((1, d // 2, tn), lambda l, j: (l, 1, j)),
            pl.BlockSpec((1, 1, tn), lambda l, j: (l, 0, j)),
        ],
        out_specs=pl.BlockSpec((1, b, tn), lambda l, j: (l, 0, j)),
        out_shape=jax.ShapeDtypeStruct((depth, b, n), F32),
        compiler_params=_params(("parallel", "parallel")),
        name="adaln",
    )(c, ada_w, ada_w, ada_b.reshape(depth, 1, n))


def _ffn_kernel(*refs, k0, n_ff, final, n_cast):
    x_ref, mod_ref, nw_ref, wg_ref, wu_ref, wd_ref, fw_ref = refs[:7]
    srcs = refs[7:7 + n_cast]
    o_ref = refs[7 + n_cast]
    dsts = refs[8 + n_cast:8 + 2 * n_cast]
    h_ref = refs[8 + 2 * n_cast]
    for src, dst in zip(srcs, dsts):
        dst[...] = src[...].astype(BF16)
    j = pl.program_id(1)

    def accumulate(base_ref):
        half_gate = 0.5 * mod_ref[0, k0 + 2:k0 + 3, :]
        rows = h_ref.shape[0] // 2
        for r in range(2):
            sl = slice(r * rows, (r + 1) * rows)
            h = h_ref[sl, :]
            g = _bdot(h, wg_ref[...])
            u = _bdot(h, wu_ref[...])
            o_ref[sl, :] = base_ref[sl, :] + half_gate * _bdot((_silu(g) * u).astype(BF16), wd_ref[...])

    @pl.when(j == 0)
    def _():
        h = _norm_mod(x_ref[...], nw_ref[...], mod_ref[0, k0:k0 + 1, :], mod_ref[0, k0 + 1:k0 + 2, :])
        h_ref[...] = h.astype(BF16)
        accumulate(x_ref)

    @pl.when(j > 0)
    def _():
        accumulate(o_ref)

    if final:
        @pl.when(j == n_ff - 1)
        def _():
            xn = o_ref[...]
            o_ref[...] = xn * lax.rsqrt(jnp.mean(xn * xn, axis=-1, keepdims=True) + EPS) * fw_ref[...]


def _ffn(x, mod, nw, w_gu, w_down, fw, *, layer, half, seq, final, next_f32=None, extra_f32=(), tm=1024, tf=512):
    t, d = x.shape
    dff = w_down.shape[0]
    n_i, n_ff = t // tm, dff // tf
    tpb = seq // tm
    k0 = 6 * half
    cast_next = next_f32 is not None
    n_cast = (2 if cast_next else 0) + len(extra_f32)
    kern = functools.partial(_ffn_kernel, k0=k0, n_ff=n_ff, final=final, n_cast=n_cast)
    in_specs = [
        pl.BlockSpec((tm, d), lambda i, j: (i, 0)),
        pl.BlockSpec((1, N_MOD, d), lambda i, j: (i // tpb, 0, 0)),
        pl.BlockSpec((None, None, 1, d), lambda i, j: (layer, 2 * half, 0, 0)),
        pl.BlockSpec((d, tf), lambda i, j: (0, j)),
        pl.BlockSpec((d, tf), lambda i, j: (0, j + n_ff)),
        pl.BlockSpec((tf, d), lambda i, j: (j, 0)),
        pl.BlockSpec((1, d), lambda i, j: (0, 0)),
    ]
    out_specs = [pl.BlockSpec((tm, d), lambda i, j: (i, 0))]
    out_shape = [jax.ShapeDtypeStruct((t, d), F32)]
    args = [x, mod, nw, w_gu, w_gu, w_down, fw]
    if cast_next:
        gu32, down32, nl, nh = next_f32
        gu_blk = (d // n_i, 2 * dff // n_ff)
        down_blk = (dff // n_ff, d // n_i)
        in_specs += [pl.BlockSpec((None, None) + gu_blk, lambda i, j: (nl, nh, i, j)),
                     pl.BlockSpec((None, None) + down_blk, lambda i, j: (nl, nh, j, i))]
        out_specs += [pl.BlockSpec(gu_blk, lambda i, j: (i, j)), pl.BlockSpec(down_blk, lambda i, j: (j, i))]
        out_shape += [jax.ShapeDtypeStruct((d, 2 * dff), BF16), jax.ShapeDtypeStruct((dff, d), BF16)]
        args += [gu32, down32]
    cast_args = []
    for w32 in extra_f32:
        rows, cols = w32.shape
        blk = (rows // n_i, cols // EXTRA_COL_BLOCKS)
        spec = pl.BlockSpec(blk, lambda i, j: (i, jnp.minimum(j, EXTRA_COL_BLOCKS - 1)))
        in_specs.append(spec)
        out_specs.append(spec)
        out_shape.append(jax.ShapeDtypeStruct((rows, cols), BF16))
        cast_args.append(w32)
    args += cast_args
    outs = pl.pallas_call(
        kern,
        grid=(n_i, n_ff),
        in_specs=in_specs,
        out_specs=out_specs,
        out_shape=out_shape,
        scratch_shapes=[pltpu.VMEM((tm, d), BF16)],
        compiler_params=_params(("parallel", "arbitrary"), vmem=VMEM_LIMIT_BIG),
        name="ffn",
    )(*args)
    return outs if n_cast else outs[0]


def _abin_kernel(x_ref, mod_ref, nw_ref, w_ref, wf_ref, bf_ref, pw_ref, ps_ref,
                 pool_ref, qkv_ref, cum_ref, h_ref, ubuf, fcarry, *, k0, tm, tpb, gw, tn):
    i = pl.program_id(0)
    first = (i % tpb) == 0

    h = _norm_mod(x_ref[...], nw_ref[...], mod_ref[0, k0:k0 + 1, :], mod_ref[0, k0 + 1:k0 + 2, :])
    h_ref[...] = h.astype(BF16)
    hb = h_ref[...]

    for part in range(3):
        acc = _bdot(hb, w_ref[:, (part + 1) * tn:(part + 2) * tn])
        if part == 0:
            acc = acc * (LOG2E * HEAD_DIM ** -0.5)
        qkv_ref[:, part * tn:(part + 1) * tn] = acc.astype(BF16)

    f = _bdot(hb, wf_ref[...]) + bf_ref[...]
    log_f = jnp.minimum(f, 0.0) - jnp.log1p(jnp.exp(-jnp.abs(f)))
    r = lax.broadcasted_iota(jnp.int32, (tm, tm), 0)
    c = lax.broadcasted_iota(jnp.int32, (tm, tm), 1)
    tri = jnp.where(r >= c, 1.0, 0.0).astype(BF16)

    @pl.when(first)
    def _():
        fcarry[...] = jnp.zeros_like(fcarry)
        ubuf[0:POOL_HALO, :] = jnp.zeros((POOL_HALO, ubuf.shape[1]), F32)

    cum = _tri_cumsum(tri, log_f) + fcarry[...]
    cum_ref[...] = cum
    fcarry[...] = cum[tm - 1:tm, :]

    u = _bdot(hb, w_ref[:, 0:tn])
    ubuf[POOL_HALO:POOL_HALO + tm, :] = u
    pos = (i % tpb) * tm + lax.broadcasted_iota(jnp.int32, (tm, 1), 0)
    for g, win in enumerate(POOL_WINDOWS):
        sl = slice(g * gw, (g + 1) * gw)
        acc = u[:, sl]
        for s in range(1, win):
            acc = acc + ubuf[POOL_HALO - s:POOL_HALO - s + tm, sl]
        cnt = jnp.minimum(pos + 1, win).astype(F32)
        dlt = acc / cnt - u[:, sl]
        y = _bdot(dlt.astype(BF16), pw_ref[g]) * ps_ref[:, sl]
        pool_ref[:, sl] = y.astype(BF16)
    ubuf[0:POOL_HALO, :] = ubuf[tm:tm + POOL_HALO, :]


def _abin(x, mod, nw, w_main, w_f, b_f, pool_w, pool_scale, *, k0, seq, tm=512):
    t, d = x.shape
    tn = pool_scale.shape[1]
    tpb = seq // tm
    groups = pool_w.shape[0]
    gw = tn // groups
    kern = functools.partial(_abin_kernel, k0=k0, tm=tm, tpb=tpb, gw=gw, tn=tn)
    return pl.pallas_call(
        kern,
        grid=(t // tm,),
        in_specs=[
            pl.BlockSpec((tm, d), lambda i: (i, 0)),
            pl.BlockSpec((1, N_MOD, d), lambda i: (i // tpb, 0, 0)),
            pl.BlockSpec((1, d), lambda i: (0, 0)),
            pl.BlockSpec(w_main.shape, lambda i: (0, 0), pipeline_mode=pl.Buffered(1)),
            pl.BlockSpec((d, LANES), lambda i: (0, 0)),
            pl.BlockSpec((1, LANES), lambda i: (0, 0)),
            pl.BlockSpec((groups, gw, gw), lambda i: (0, 0, 0)),
            pl.BlockSpec((1, tn), lambda i: (0, 0)),
        ],
        out_specs=[
            pl.BlockSpec((tm, tn), lambda i: (i, 0)),
            pl.BlockSpec((tm, 3 * tn), lambda i: (i, 0)),
            pl.BlockSpec((tm, LANES), lambda i: (i, 0)),
        ],
        out_shape=[
            jax.ShapeDtypeStruct((t, tn), BF16),
            jax.ShapeDtypeStruct((t, 3 * tn), BF16),
            jax.ShapeDtypeStruct((t, LANES), F32),
        ],
        scratch_shapes=[
            pltpu.VMEM((tm, d), BF16),
            pltpu.VMEM((POOL_HALO + tm, tn), F32),
            pltpu.VMEM((1, LANES), F32),
        ],
        compiler_params=_params(("arbitrary",)),
        name="abin",
    )(x, mod, nw, w_main, w_f, b_f, pool_w, pool_scale)


def _fox_kernel(q_ref, k_ref, v_ref, cum_ref, o_ref, qa_ref, ka_ref, va_ref, *, tq, seq):
    h = pl.program_id(1)
    hd = HEAD_DIM
    cum = cum_ref[0]
    lane = lax.broadcasted_iota(jnp.int32, cum.shape, 1)
    f = jnp.sum(jnp.where(lane == h, cum, 0.0), axis=-1, keepdims=True) * LOG2E
    f_hi = f.astype(BF16).astype(F32)
    r1 = f - f_hi
    f_mid = r1.astype(BF16).astype(F32)
    f_lo = r1 - f_mid
    zero = jnp.zeros_like(cum)
    one_or_zero = jnp.where(lane < 6, 1.0, 0.0)
    fq = jnp.where(lane == 0, f_hi, jnp.where(lane == 1, f_mid, jnp.where(lane == 2, f_lo, one_or_zero)))
    fk = jnp.where(lane < 3, 1.0,
                   jnp.where(lane == 3, -f_hi, jnp.where(lane == 4, -f_mid, jnp.where(lane == 5, -f_lo, zero))))
    qa_ref[:, 0:hd] = q_ref[0]
    qa_ref[:, hd:2 * hd] = fq.astype(BF16)
    ka_ref[:, 0:hd] = k_ref[0]
    ka_ref[:, hd:2 * hd] = fk.astype(BF16)
    va_ref[:, 0:hd] = v_ref[0]
    va_ref[:, hd:2 * hd] = jnp.ones((seq, hd), BF16)

    r = lax.broadcasted_iota(jnp.int32, (tq, tq), 0)
    c = lax.broadcasted_iota(jnp.int32, (tq, tq), 1)
    causal = c <= r

    nq = seq // tq
    m = [jnp.full((tq, 1), NEG_BIG, F32) for _ in range(nq)]
    acc = [jnp.zeros((tq, 2 * hd), F32) for _ in range(nq)]
    for j in range(nq):
        ka = ka_ref[j * tq:(j + 1) * tq, :]
        va = va_ref[j * tq:(j + 1) * tq, :]
        for qi in range(j, nq):
            s = lax.dot_general(qa_ref[qi * tq:(qi + 1) * tq, :], ka, (((1,), (1,)), ((), ())),
                                preferred_element_type=F32)
            if qi == j:
                s = jnp.where(causal, s, NEG_BIG)
            m_new = jnp.maximum(m[qi], jnp.max(s, axis=-1, keepdims=True))
            p = jnp.exp2(s - m_new)
            acc[qi] = jnp.exp2(m[qi] - m_new) * acc[qi] + _bdot(p.astype(BF16), va)
            m[qi] = m_new
    for qi in range(nq):
        o_ref[0, qi * tq:(qi + 1) * tq, :] = (acc[qi][:, 0:hd] / acc[qi][:, hd:2 * hd]).astype(BF16)


def _fox(qkv, cum, *, batch, seq, heads, tq=256):
    qkv3 = qkv.reshape(batch, seq, 3 * heads * HEAD_DIM)
    cum3 = cum.reshape(batch, seq, LANES)
    kern = functools.partial(_fox_kernel, tq=tq, seq=seq)
    return pl.pallas_call(
        kern,
        grid=(batch, heads),
        in_specs=[
            pl.BlockSpec((1, seq, HEAD_DIM), lambda b, h: (b, 0, h)),
            pl.BlockSpec((1, seq, HEAD_DIM), lambda b, h: (b, 0, heads + h)),
            pl.BlockSpec((1, seq, HEAD_DIM), lambda b, h: (b, 0, 2 * heads + h)),
            pl.BlockSpec((1, seq, LANES), lambda b, h: (b, 0, 0)),
        ],
        out_specs=pl.BlockSpec((1, seq, HEAD_DIM), lambda b, h: (b, 0, h)),
        out_shape=jax.ShapeDtypeStruct((batch, seq, heads * HEAD_DIM), BF16),
        scratch_shapes=[pltpu.VMEM((seq, 2 * HEAD_DIM), BF16)] * 3,
        compiler_params=_params(("parallel", "parallel")),
        name="fox",
    )(qkv3, qkv3, qkv3, cum3)


def _proj_res_kernel(*refs, n_in, gate_row):
    x_ref, mod_ref = refs[0], refs[1]
    a_refs = refs[2:2 + n_in]
    w_ref, o_ref = refs[2 + n_in], refs[3 + n_in]
    y = None
    row = 0
    for a_ref in a_refs:
        ka = a_ref.shape[1]
        part = _bdot(a_ref[...], w_ref[row:row + ka, :])
        y = part if y is None else y + part
        row += ka
    o_ref[...] = x_ref[...] + mod_ref[0, gate_row:gate_row + 1, :] * y


def _proj_res(x, mod, acts, weight, *, gate_row, seq, tm=512):
    t, d = x.shape
    tpb = seq // tm
    n_in = len(acts)
    kern = functools.partial(_proj_res_kernel, n_in=n_in, gate_row=gate_row)
    in_specs = [
        pl.BlockSpec((tm, d), lambda i: (i, 0)),
        pl.BlockSpec((1, N_MOD, d), lambda i: (i // tpb, 0, 0)),
    ]
    in_specs += [pl.BlockSpec((tm, a.shape[1]), lambda i: (i, 0)) for a in acts]
    in_specs += [pl.BlockSpec(weight.shape, lambda i: (0, 0), pipeline_mode=pl.Buffered(1))]
    return pl.pallas_call(
        kern,
        grid=(t // tm,),
        in_specs=in_specs,
        out_specs=pl.BlockSpec((tm, d), lambda i: (i, 0)),
        out_shape=jax.ShapeDtypeStruct((t, d), F32),
        compiler_params=_params(("parallel",)),
        name="proj_res",
    )(x, mod, *acts, weight)


def _gdnin_kernel(x_ref, mod_ref, nw_ref, w_ref, wg_ref, alog_ref, dtb_ref, cw_ref,
                  o_ref, gate_ref, h_ref, pbuf, carry_ref, *, k0, tm, tpb, n_conv, n_qk, n_q):
    i = pl.program_id(0)
    n = pl.program_id(1)
    first = (i % tpb) == 0

    @pl.when(n == 0)
    def _():
        h = _norm_mod(x_ref[...], nw_ref[...], mod_ref[0, k0:k0 + 1, :], mod_ref[0, k0 + 1:k0 + 2, :])
        h_ref[...] = h.astype(BF16)
        ba = _bdot(h_ref[...], wg_ref[...])
        lane = lax.broadcasted_iota(jnp.int32, ba.shape, 1) % LANES
        g = -jnp.exp(alog_ref[...]) * _softplus(ba + dtb_ref[...])
        gate_ref[...] = jnp.where(lane < GDN_GROUP, g, jax.nn.sigmoid(ba))

        @pl.when(i == 0)
        def _():
            carry_ref[...] = jnp.zeros_like(carry_ref)

    def conv_silu():
        acc = _bdot(h_ref[...], w_ref[...])
        pbuf[0:CONV_HALO, :] = jnp.where(first, 0.0, carry_ref[n])
        pbuf[CONV_HALO:CONV_HALO + tm, :] = acc
        carry_ref[n] = acc[tm - CONV_HALO:tm, :]
        y = cw_ref[CONV_K - 1:CONV_K, :] * acc
        for kk in range(CONV_K - 1):
            lo = CONV_HALO - (CONV_K - 1) + kk
            y = y + cw_ref[kk:kk + 1, :] * pbuf[lo:lo + tm, :]
        return _silu(y)

    @pl.when(n < n_qk)
    def _():
        y = conv_silu()
        qscale = jnp.where(n < n_q, HEAD_DIM ** -0.5, 1.0).astype(F32)
        for hh in range(y.shape[1] // HEAD_DIM):
            seg = y[:, hh * HEAD_DIM:(hh + 1) * HEAD_DIM]
            inv = lax.rsqrt(jnp.sum(seg * seg, axis=-1, keepdims=True) + EPS) * qscale
            o_ref[:, hh * HEAD_DIM:(hh + 1) * HEAD_DIM] = (seg * inv).astype(BF16)

    @pl.when((n >= n_qk) & (n < n_conv))
    def _():
        o_ref[...] = conv_silu().astype(BF16)

    @pl.when(n >= n_conv)
    def _():
        o_ref[...] = _bdot(h_ref[...], w_ref[...]).astype(BF16)


def _gdnin(x, mod, nw, w_main, w_gate, alog_l, dtb_l, conv_w, *, k0, seq, qk_dim, conv_ch, n_main,
           tm=512, tn=2048):
    t, d = x.shape
    n_total = n_main
    tpb = seq // tm
    n_conv = conv_ch // tn
    n_qk = 2 * qk_dim // tn
    n_q = qk_dim // tn
    gl = w_gate.shape[1]
    kern = functools.partial(_gdnin_kernel, k0=k0, tm=tm, tpb=tpb, n_conv=n_conv, n_qk=n_qk, n_q=n_q)
    return pl.pallas_call(
        kern,
        grid=(t // tm, n_total // tn),
        in_specs=[
            pl.BlockSpec((tm, d), lambda i, n: (i, 0)),
            pl.BlockSpec((1, N_MOD, d), lambda i, n: (i // tpb, 0, 0)),
            pl.BlockSpec((1, d), lambda i, n: (0, 0)),
            pl.BlockSpec((d, tn), lambda i, n: (0, n)),
            pl.BlockSpec((d, gl), lambda i, n: (0, 0)),
            pl.BlockSpec((1, gl), lambda i, n: (0, 0)),
            pl.BlockSpec((1, gl), lambda i, n: (0, 0)),
            pl.BlockSpec((CONV_K, tn), lambda i, n: (0, jnp.minimum(n, n_conv - 1))),
        ],
        out_specs=[
            pl.BlockSpec((tm, tn), lambda i, n: (i, n)),
            pl.BlockSpec((tm, gl), lambda i, n: (i, 0)),
        ],
        out_shape=[
            jax.ShapeDtypeStruct((t, n_total), BF16),
            jax.ShapeDtypeStruct((t, gl), F32),
        ],
        scratch_shapes=[
            pltpu.VMEM((tm, d), BF16),
            pltpu.VMEM((CONV_HALO + tm, tn), F32),
            pltpu.VMEM((n_conv, CONV_HALO, tn), F32),
        ],
        compiler_params=_params(("arbitrary", "arbitrary")),
        name="gdnin",
    )(x, mod, nw, w_main, w_gate, alog_l, dtb_l, conv_w)


def _gdn_kernel(q_ref, k_ref, v_ref, z_ref, gt_ref, nw_ref, o_ref, state_ref, *, n_chunks):
    @pl.when(pl.program_id(2) == 0)
    def _():
        state_ref[...] = jnp.zeros_like(state_ref)

    c = CHUNK
    hd = HEAD_DIM
    row = lax.broadcasted_iota(jnp.int32, (c, 2 * c), 0)
    lane = lax.broadcasted_iota(jnp.int32, (c, 2 * c), 1)
    col = lane % c
    left = lane < c
    tril = row >= col
    strict = row > col
    eye2 = (row == col).astype(F32)
    br = lax.broadcasted_iota(jnp.int32, (2 * c, 2 * c), 0) // c
    bc = lax.broadcasted_iota(jnp.int32, (2 * c, 2 * c), 1) // c
    same_block = br == bc

    def blockdiag(p):
        return jnp.where(same_block, jnp.concatenate([p, p], axis=0), 0.0)

    def split(v):
        head = v.astype(BF16)
        return head, (v - head.astype(F32)).astype(BF16)

    nw = nw_ref[...]
    zeros_b = jnp.zeros((c, hd), BF16)
    n_pairs = GDN_GROUP // 2
    items = [(cc, jq) for cc in range(n_chunks) for jq in range(n_pairs)]

    ts = n_chunks * c
    raw = gt_ref[0]
    rr = lax.broadcasted_iota(jnp.int32, (ts, ts), 0)
    rc = lax.broadcasted_iota(jnp.int32, (ts, ts), 1)
    tri = jnp.where((rr >= rc) & ((rr // c) == (rc // c)), 1.0, 0.0).astype(BF16)
    gates_all = jnp.where(lax.broadcasted_iota(jnp.int32, raw.shape, 1) < GDN_GROUP, _tri_cumsum(tri, raw), raw)

    chunk = []
    for cc in range(n_chunks):
        rows = slice(cc * c, (cc + 1) * c)
        gates = gates_all[rows, :]
        g_last = gates[c - 1:c, :]
        chunk.append(dict(
            rows=rows, gates=gates,
            gates_t=jnp.concatenate([gates, gates], axis=0).T,
            exp_g=jnp.exp(gates),
            exp_rest=jnp.exp(g_last - gates),
            exp_last=jnp.exp(g_last)))

    it = {}
    for key in items:
        cc, jq = key
        ck = chunk[cc]
        gates, gates_t, rows = ck["gates"], ck["gates_t"], ck["rows"]
        ha, hb = 2 * jq, 2 * jq + 1
        q = q_ref[0, rows, jq * hd:(jq + 1) * hd]
        k = k_ref[0, rows, jq * hd:(jq + 1) * hd]
        kq = lax.dot_general(jnp.concatenate([k, q], axis=0), jnp.concatenate([k, k], axis=0),
                             (((1,), (1,)), ((), ())), preferred_element_type=F32)
        g_col = jnp.where(left, gates[:, ha:ha + 1], gates[:, hb:hb + 1])
        g_row = jnp.where(left[0:1], gates_t[ha:ha + 1, :], gates_t[hb:hb + 1, :])
        beta_a = gates[:, GDN_GROUP + ha:GDN_GROUP + ha + 1]
        beta_b = gates[:, GDN_GROUP + hb:GDN_GROUP + hb + 1]
        decay = jnp.where(tril, jnp.exp(g_col - g_row), 0.0)
        a_mat = jnp.where(strict, jnp.where(left, beta_a, beta_b) * kq[:c] * decay, 0.0)
        it[key] = dict(q=q, k=k, attn=(kq[c:] * decay).astype(BF16), a=a_mat, p=-a_mat, x=eye2 - a_mat,
                       heads=((ha, beta_a), (hb, beta_b)))

    for _ in range(5):
        for key in items:
            d = it[key]
            d["p"] = _bdot(d["p"].astype(BF16), blockdiag(d["p"]).astype(BF16))
        for key in items:
            d = it[key]
            d["x"] = d["x"] + _bdot(d["x"].astype(BF16), blockdiag(d["p"]).astype(BF16))

    for key in items:
        d = it[key]
        a_hi, a_lo = split(eye2 + d["a"])
        x_hi, x_lo = split(blockdiag(d["x"]))
        prod = _bdot(jnp.concatenate([a_hi, a_lo, a_hi], axis=1), jnp.concatenate([x_hi, x_hi, x_lo], axis=0))
        d["r"] = eye2 - prod
    for key in items:
        d = it[key]
        d["x"] = d["x"] + _bdot(d["x"].astype(BF16), blockdiag(d["r"]).astype(BF16))

    for key in items:
        d = it[key]
        ck = chunk[key[0]]
        kf = d["k"].astype(F32)
        vbs, kbs = [], []
        for h, beta in d["heads"]:
            vf = v_ref[0, ck["rows"], h * hd:(h + 1) * hd].astype(F32)
            vbs.append((vf * beta).astype(BF16))
            kbs.append((kf * (beta * ck["exp_g"][:, h:h + 1])).astype(BF16))
        rhs = jnp.concatenate([
            jnp.concatenate([vbs[0], zeros_b, kbs[0], zeros_b], axis=1),
            jnp.concatenate([zeros_b, vbs[1], zeros_b, kbs[1]], axis=1)], axis=0)
        d["uw"] = _bdot(d["x"].astype(BF16), rhs)
        d["kg"] = jnp.concatenate([(kf * ck["exp_rest"][:, h:h + 1]).astype(BF16) for h, _ in d["heads"]], axis=0)
        qf = d["q"].astype(F32)
        d["qg"] = [(qf * ck["exp_g"][:, h:h + 1]).astype(BF16) for h, _ in d["heads"]]

    states = [state_ref[h] for h in range(GDN_GROUP)]
    for cc in range(n_chunks):
        ck = chunk[cc]
        wqs = {}
        for jq in range(n_pairs):
            d = it[(cc, jq)]
            for r, (h, _) in enumerate(d["heads"]):
                w_r = d["uw"][:, (2 + r) * hd:(3 + r) * hd].astype(BF16)
                wqs[h] = _bdot(jnp.concatenate([w_r, d["qg"][r]], axis=0), states[h].astype(BF16))
        outs = {}
        for jq in range(n_pairs):
            d = it[(cc, jq)]
            (ha, _), (hb, _) = d["heads"]
            vna = (d["uw"][:, 0:hd] - wqs[ha][:c]).astype(BF16)
            vnb = (d["uw"][:, hd:2 * hd] - wqs[hb][:c]).astype(BF16)
            vn = jnp.concatenate([jnp.concatenate([vna, zeros_b], axis=1),
                                  jnp.concatenate([zeros_b, vnb], axis=1)], axis=0)
            o_intra = _bdot(d["attn"], vn)
            ds = lax.dot_general(d["kg"], vn, (((0,), (0,)), ((), ())), preferred_element_type=F32)
            for r, h in enumerate((ha, hb)):
                states[h] = states[h] * ck["exp_last"][0:1, h:h + 1] + ds[:, r * hd:(r + 1) * hd]
                outs[h] = wqs[h][c:] + o_intra[:, r * hd:(r + 1) * hd]
        for h in range(GDN_GROUP):
            o = outs[h]
            on = o * lax.rsqrt(jnp.mean(o * o, axis=-1, keepdims=True) + EPS) * nw
            zf = z_ref[0, ck["rows"], h * hd:(h + 1) * hd].astype(F32)
            o_ref[0, ck["rows"], h * hd:(h + 1) * hd] = (on * _silu(zf)).astype(BF16)

    for h in range(GDN_GROUP):
        state_ref[h] = states[h]


def _gdn(qkvz, gates, norm_w, *, batch, seq, qk_dim, v_dim, n_chunks=4):
    total = qkvz.shape[1]
    x3 = qkvz.reshape(batch, seq, total)
    g3 = gates.reshape(batch, seq, gates.shape[1])
    hd = HEAD_DIM
    qw = GDN_GROUP // 2 * hd
    vw = GDN_GROUP * hd
    n_groups = v_dim // vw
    k_off = qk_dim // qw
    v_off = 2 * qk_dim // vw
    z_off = (2 * qk_dim + v_dim) // vw
    ts = n_chunks * CHUNK
    return pl.pallas_call(
        functools.partial(_gdn_kernel, n_chunks=n_chunks),
        grid=(batch, n_groups, seq // ts),
        in_specs=[
            pl.BlockSpec((1, ts, qw), lambda b, g, c: (b, c, g)),
            pl.BlockSpec((1, ts, qw), lambda b, g, c: (b, c, k_off + g)),
            pl.BlockSpec((1, ts, vw), lambda b, g, c: (b, c, v_off + g)),
            pl.BlockSpec((1, ts, vw), lambda b, g, c: (b, c, z_off + g)),
            pl.BlockSpec((1, ts, LANES), lambda b, g, c: (b, c, g)),
            pl.BlockSpec((1, hd), lambda b, g, c: (0, 0)),
        ],
        out_specs=pl.BlockSpec((1, ts, vw), lambda b, g, c: (b, c, g)),
        out_shape=jax.ShapeDtypeStruct((batch, seq, v_dim), BF16),
        scratch_shapes=[pltpu.VMEM((GDN_GROUP, hd, hd), F32)],
        compiler_params=_params(("parallel", "parallel", "arbitrary")),
        name="gdn",
    )(x3, x3, x3, x3, g3, norm_w)


def _grouped_gate_layout(w_b, w_a, a_log, dt_bias):
    d, hv = w_a.shape
    ng = hv // GDN_GROUP
    pad = LANES - 2 * GDN_GROUP
    wa = w_a.reshape(d, ng, GDN_GROUP)
    wb = w_b.reshape(d, ng, GDN_GROUP)
    w = jnp.concatenate([wa, wb, jnp.zeros((d, ng, pad), w_a.dtype)], axis=-1).reshape(d, ng * LANES)
    zg = jnp.zeros((ng, LANES - GDN_GROUP), F32)
    alog = jnp.concatenate([a_log.reshape(ng, GDN_GROUP).astype(F32), zg], axis=-1).reshape(1, ng * LANES)
    dtb = jnp.concatenate([dt_bias.reshape(ng, GDN_GROUP).astype(F32), zg], axis=-1).reshape(1, ng * LANES)
    return w.astype(BF16), alog, dtb


def kernel(x, c, ada_w, ada_b, norm_w, ffn_w_gu, ffn_w_down, ab_w_in, ab_b_f, pool_w, pool_scale, ab_w_out,
           gdn_w_in, gdn_conv_w, gdn_a_log, gdn_dt_bias, gdn_norm_w, gdn_w_out, final_norm_w):
    batch, seq, d = x.shape
    depth = ada_w.shape[0]
    t = batch * seq

    mod_all = _adaln(c, ada_w, ada_b).reshape(depth, batch, N_MOD, d)
    w_gu = ffn_w_gu[0, 0].astype(BF16)
    w_down = ffn_w_down[0, 0].astype(BF16)

    out_w32 = [ab_w_out[e] for e in range(ab_w_out.shape[0])] + [gdn_w_out[e] for e in range(gdn_w_out.shape[0])]
    out_w16 = []

    def ffn(xf, mod, w_gu, w_down, layer, half):
        last = layer == depth - 1 and half == 1
        first_call = layer == 0 and half == 0
        nxt = None if last else (ffn_w_gu, ffn_w_down, layer + half, 1 - half)
        out = _ffn(xf, mod, nw4, w_gu, w_down, fw, layer=layer, half=half, seq=seq, final=last, next_f32=nxt,
                   extra_f32=out_w32 if first_call else ())
        if last:
            return out, None, None
        if first_call:
            out_w16.extend(out[3:])
        return out[:3]
    fw = final_norm_w.reshape(1, d)
    xf = x.reshape(t, d)

    nw4 = norm_w.reshape(depth, 3, 1, d)
    for layer in range(depth):
        mod = mod_all[layer]
        e = layer // 2
        nws = nw4[layer]
        xf, w_gu, w_down = ffn(xf, mod, w_gu, w_down, layer, 0)

        if layer % 2 == 0:
            pool_width = pool_scale.shape[1]
            fox_heads = ab_b_f.shape[1]
            fox_width = fox_heads * HEAD_DIM
            n_main = pool_width + 3 * fox_width
            w_in = ab_w_in[e]
            w_f = jnp.pad(w_in[:, n_main:], ((0, 0), (0, LANES - fox_heads))).astype(BF16)
            b_f = jnp.pad(ab_b_f[e], (0, LANES - fox_heads)).reshape(1, LANES)
            pool_out, qkv, cum = _abin(xf, mod, nws[1], w_in.astype(BF16), w_f, b_f, pool_w[e].astype(BF16),
                                       pool_scale[e].reshape(1, pool_width), k0=3, seq=seq)
            attn = _fox(qkv, cum, batch=batch, seq=seq, heads=fox_heads)
            w_out = out_w16[e]
            xf = _proj_res(xf, mod, [pool_out, attn.reshape(t, fox_width)], w_out, gate_row=5, seq=seq)
        else:
            v_heads = gdn_a_log.shape[1]
            v_dim = v_heads * HEAD_DIM
            conv_ch = gdn_conv_w.shape[2]
            qk_dim = (conv_ch - v_dim) // 2
            n_main = conv_ch + v_dim
            w_in = gdn_w_in[e]
            w_gate, alog_l, dtb_l = _grouped_gate_layout(
                w_in[:, n_main:n_main + v_heads], w_in[:, n_main + v_heads:], gdn_a_log[e], gdn_dt_bias[e])
            qkvz, gates = _gdnin(xf, mod, nws[1], w_in.astype(BF16), w_gate, alog_l, dtb_l,
                                 gdn_conv_w[e], k0=3, seq=seq, qk_dim=qk_dim, conv_ch=conv_ch, n_main=n_main)
            o = _gdn(qkvz, gates, gdn_norm_w[e].reshape(1, HEAD_DIM), batch=batch, seq=seq,
                     qk_dim=qk_dim, v_dim=v_dim)
            xf = _proj_res(xf, mod, [o.reshape(t, v_dim)], out_w16[ab_w_out.shape[0] + e], gate_row=5, seq=seq)

        xf, w_gu, w_down = ffn(xf, mod, w_gu, w_down, layer, 1)
    return xf.reshape(batch, seq, d)
```
